```python
import jax, jax.numpy as jnp
from jax import lax
import numpy as np

D_MODEL = 1024
BATCH = 32
SEQ = 2048
DEPTH = 2

N_HEADS = 8
HEAD_DIM = 64
KV_LATENT = 128
ATTN_SCALE = HEAD_DIM ** -0.5
IDX_HEADS = 8
IDX_DIM = 64
IDX_SCALE = (IDX_HEADS ** -0.5) * (IDX_DIM ** -0.5)
TOPK_MAX = 256
Q_BLOCK = 128
POOL_WINDOWS = (2, 4, 8, 16)
POOL_GROUP = 128
POOL_WIDTH = POOL_GROUP * len(POOL_WINDOWS)
D_FF = 2816
EPS = 1e-6
SPLITS = (N_HEADS * HEAD_DIM, KV_LATENT, IDX_HEADS * IDX_DIM, IDX_DIM, IDX_HEADS, POOL_WIDTH, D_MODEL, D_MODEL)
D_IN = sum(SPLITS)

kernel_name = 'hybrid_dsa_pool_macaron'


def rms_norm(x, g):
    xf = x.astype(jnp.float32)
    y = xf * lax.rsqrt(jnp.mean(xf * xf, axis=-1, keepdims=True) + EPS)
    return (y * g.astype(jnp.float32)).astype(x.dtype)


def swiglu(h, wg, wu, wd):
    return (jax.nn.silu(h @ wg) * (h @ wu)) @ wd


def dsa_attention(q_lat, c_kv, q_idx, k_idx, w_idx):
    B, T = c_kv.shape[0], c_kv.shape[1]
    k_sel = min(TOPK_MAX, T // 4)
    n_blk = T // Q_BLOCK
    key_pos = jnp.arange(T, dtype=jnp.int32)

    def to_blocks(a):
        return jnp.moveaxis(a.reshape((B, n_blk, Q_BLOCK) + a.shape[2:]), 1, 0)

    def one_block(args):
        ql, qi, wi, start = args
        q_pos = start + jnp.arange(Q_BLOCK, dtype=jnp.int32)
        causal = key_pos[None, :] <= q_pos[:, None]
        rel = jax.nn.relu(jnp.einsum('bqjd,bsd->bqjs', qi, k_idx).astype(jnp.float32))
        score = jnp.einsum('bqjs,bqj->bqs', rel, wi.astype(jnp.float32))
        score = jnp.where(causal[None], score, -jnp.inf)
        _, idx = lax.top_k(score, k_sel)
        valid = idx <= q_pos[None, :, None]
        c_sel = jax.vmap(lambda c, i: c[i])(c_kv, idx)
        logits = jnp.einsum('bqhc,bqkc->bqhk', ql, c_sel).astype(jnp.float32) * ATTN_SCALE
        logits = jnp.where(valid[:, :, None, :], logits, -jnp.inf)
        p = jax.nn.softmax(logits, axis=-1).astype(c_sel.dtype)
        return jnp.einsum('bqhk,bqkc->bqhc', p, c_sel)

    starts = jnp.arange(n_blk, dtype=jnp.int32) * Q_BLOCK
    o = lax.map(one_block, (to_blocks(q_lat), to_blocks(q_idx), to_blocks(w_idx), starts))
    return jnp.moveaxis(o, 0, 1).reshape(B, T, N_HEADS, KV_LATENT)


def multiscale_pool(p):
    B, T = p.shape[0], p.shape[1]
    pf = p.astype(jnp.float32).reshape(B, T, len(POOL_WINDOWS), POOL_GROUP)
    cs = jnp.cumsum(pf, axis=1)
    t1 = jnp.arange(1, T + 1, dtype=jnp.float32)
    outs = []
    for g, w in enumerate(POOL_WINDOWS):
        c = cs[:, :, g]
        lag = jnp.pad(c, ((0, 0), (w, 0), (0, 0)))[:, :T]
        cnt = jnp.minimum(t1, float(w))[None, :, None]
        outs.append((c - lag) / cnt - pf[:, :, g])
    return jnp.stack(outs, axis=2).astype(p.dtype)


def setup_inputs(seed: int = 0) -> dict:
    key = jax.random.key(seed)
    ks = jax.random.split(key, 24)
    f32 = jnp.float32

    def nrm(k, shape, fan_in):
        return jax.random.normal(k, shape, f32) * (fan_in ** -0.5)

    def gain(k, shape):
        return 1.0 + 0.05 * jax.random.normal(k, shape, f32)

    G = len(POOL_WINDOWS)
    return {
        'x': jax.random.normal(ks[0], (BATCH, SEQ, D_MODEL), f32),
        'norm_ffn1': gain(ks[1], (DEPTH, D_MODEL)),
        'ffn1_gate': nrm(ks[2], (DEPTH, D_MODEL, D_FF), D_MODEL),
        'ffn1_up': nrm(ks[3], (DEPTH, D_MODEL, D_FF), D_MODEL),
        'ffn1_down': nrm(ks[4], (DEPTH, D_FF, D_MODEL), D_FF),
        'norm_mix': gain(ks[5], (DEPTH, D_MODEL)),
        'w_in': nrm(ks[6], (DEPTH, D_MODEL, D_IN), D_MODEL),
        'norm_kv': gain(ks[7], (DEPTH, KV_LATENT)),
        'w_uk': nrm(ks[8], (DEPTH, KV_LATENT, N_HEADS, HEAD_DIM), KV_LATENT),
        'w_uv': nrm(ks[9], (DEPTH, KV_LATENT, N_HEADS, HEAD_DIM), KV_LATENT),
        'pool_w': nrm(ks[10], (DEPTH, G, POOL_GROUP, POOL_GROUP), POOL_GROUP),
        'pool_scale': gain(ks[11], (DEPTH, POOL_WIDTH)),
        'w_branch_attn': nrm(ks[12], (DEPTH, N_HEADS * HEAD_DIM, D_MODEL), N_HEADS * HEAD_DIM),
        'w_branch_pool': nrm(ks[13], (DEPTH, POOL_WIDTH, D_MODEL), POOL_WIDTH),
        'w_out': nrm(ks[14], (DEPTH, D_MODEL, D_MODEL), D_MODEL),
        'norm_ffn2': gain(ks[15], (DEPTH, D_MODEL)),
        'ffn2_gate': nrm(ks[16], (DEPTH, D_MODEL, D_FF), D_MODEL),
        'ffn2_up': nrm(ks[17], (DEPTH, D_MODEL, D_FF), D_MODEL),
        'ffn2_down': nrm(ks[18], (DEPTH, D_FF, D_MODEL), D_FF),
        'norm_final': gain(ks[19], (D_MODEL,)),
    }


def reference(x, norm_ffn1, ffn1_gate, ffn1_up, ffn1_down, norm_mix, w_in, norm_kv, w_uk, w_uv,
              pool_w, pool_scale, w_branch_attn, w_branch_pool, w_out, norm_ffn2, ffn2_gate,
              ffn2_up, ffn2_down, norm_final):
    B, T = x.shape[0], x.shape[1]
    cuts = np.cumsum(SPLITS)[:-1].tolist()
    for i in range(DEPTH):
        h = rms_norm(x, norm_ffn1[i])
        x = x + 0.5 * swiglu(h, ffn1_gate[i], ffn1_up[i], ffn1_down[i])

        h = rms_norm(x, norm_mix[i])
        z = h @ w_in[i]
        q, c_kv, q_idx, k_idx, w_idx, pool_in, gate_a, gate_b = jnp.split(z, cuts, axis=-1)

        q = q.reshape(B, T, N_HEADS, HEAD_DIM)
        c_kv = rms_norm(c_kv, norm_kv[i])
        q_lat = jnp.einsum('bthd,chd->bthc', q, w_uk[i])
        q_idx = q_idx.reshape(B, T, IDX_HEADS, IDX_DIM)
        o_lat = dsa_attention(q_lat, c_kv, q_idx, k_idx, w_idx * IDX_SCALE)
        attn = jnp.einsum('bthc,chd->bthd', o_lat, w_uv[i]).reshape(B, T, N_HEADS * HEAD_DIM)

        pooled = multiscale_pool(pool_in)
        mixed = jnp.einsum('btgc,gcd->btgd', pooled, pool_w[i]).reshape(B, T, POOL_WIDTH) * pool_scale[i]

        merged = (jax.nn.sigmoid(gate_a) * (attn @ w_branch_attn[i])
                  + jax.nn.sigmoid(gate_b) * (mixed @ w_branch_pool[i]))
        x = x + merged @ w_out[i]

        h = rms_norm(x, norm_ffn2[i])
        x = x + 0.5 * swiglu(h, ffn2_gate[i], ffn2_up[i], ffn2_down[i])
    return rms_norm(x, norm_final)
```

```python
import functools

import jax
import jax.numpy as jnp
from jax import lax
from jax.experimental import pallas as pl
from jax.experimental.pallas import tpu as pltpu

D_MODEL = 1024
N_HEADS = 8
HEAD_DIM = 64
KV_LATENT = 128
ATTN_SCALE = HEAD_DIM ** -0.5
IDX_HEADS = 8
IDX_DIM = 64
IDX_SCALE = (IDX_HEADS ** -0.5) * (IDX_DIM ** -0.5)
TOPK_MAX = 256
POOL_WINDOWS = (2, 4, 8, 16)
POOL_GROUP = 128
POOL_WIDTH = POOL_GROUP * len(POOL_WINDOWS)
POOL_HALO = 16
D_FF = 2816
EPS = 1e-6

LANES = 128
VMEM_LIMIT_BYTES = 56 * 1024 * 1024

TOKEN_TILE = 512
FF_CHUNK = 256
Q_TILE = 128
K_CHUNK = 128

_Z_Q = 0
_Z_CKV = _Z_Q + N_HEADS * HEAD_DIM
_Z_QI = _Z_CKV + KV_LATENT
_Z_POOL = _Z_QI + IDX_HEADS * IDX_DIM
_Z_GA = _Z_POOL + POOL_WIDTH
_Z_GB = _Z_GA + D_MODEL
_Z_TAIL = _Z_GB + D_MODEL
_Z_TOTAL = _Z_TAIL + LANES

_F32 = jnp.float32
_BF16 = jnp.bfloat16
_INT_MIN = -(2 ** 31)
_NEG_BIG = float(jnp.finfo(jnp.float32).min)


def _const_spec(shape):
    return pl.BlockSpec(shape, lambda *_: (0,) * len(shape), pipeline_mode=pl.Buffered(1))


def _rms(x, g):
    return x * lax.rsqrt(jnp.mean(x * x, axis=-1, keepdims=True) + EPS) * g


def _dot(a, b):
    return jnp.dot(a, b, preferred_element_type=_F32)


def _dot_nt(a, b):
    return lax.dot_general(a, b, (((1,), (1,)), ((), ())), preferred_element_type=_F32)


def _ffn_kernel(x_ref, g_ref, wg_ref, wu_ref, wd_ref, gf_ref, o_ref, *, final_norm):
    x = x_ref[...]
    h = _rms(x, g_ref[...]).astype(_BF16)
    acc = jnp.zeros(x.shape, _F32)
    for c in range(D_FF // FF_CHUNK):
        sl = slice(c * FF_CHUNK, (c + 1) * FF_CHUNK)
        gate = _dot(h, wg_ref[:, sl])
        up = _dot(h, wu_ref[:, sl])
        act = (gate * jax.nn.sigmoid(gate) * up).astype(_BF16)
        acc = acc + _dot(act, wd_ref[sl, :])
    y = x + 0.5 * acc
    if final_norm:
        y = _rms(y, gf_ref[...])
    o_ref[...] = y


def _ffn(x, g, wg, wu, wd, gf, final_norm):
    n = x.shape[0]
    tile = pl.BlockSpec((TOKEN_TILE, D_MODEL), lambda i: (i, 0))
    return pl.pallas_call(
        functools.partial(_ffn_kernel, final_norm=final_norm),
        grid=(n // TOKEN_TILE,),
        in_specs=[tile, _const_spec((1, D_MODEL)), _const_spec((D_MODEL, D_FF)), _const_spec((D_MODEL, D_FF)),
                  _const_spec((D_FF, D_MODEL)), _const_spec((1, D_MODEL))],
        out_specs=tile,
        out_shape=jax.ShapeDtypeStruct((n, D_MODEL), _F32),
        compiler_params=pltpu.CompilerParams(dimension_semantics=("arbitrary",), vmem_limit_bytes=VMEM_LIMIT_BYTES),
        name="ffn",
    )(x, g, wg, wu, wd, gf)


def _proj_kernel(x_ref, g_ref, w_ref, gkv_ref, wuk_ref, ql_ref, ckv_ref, qi_ref, ki_ref, wt_ref, pool_ref, sga_ref,
                 sgb_ref):
    h = _rms(x_ref[...], g_ref[...]).astype(_BF16)

    def z(lo, hi):
        return _dot(h, w_ref[:, lo:hi])

    q = z(_Z_Q, _Z_CKV).astype(_BF16)
    for hd in range(N_HEADS):
        ql = _dot(q[:, hd * HEAD_DIM:(hd + 1) * HEAD_DIM], wuk_ref[hd])
        ql_ref[0, hd] = (ql * ATTN_SCALE).astype(_BF16)
    ckv_ref[...] = _rms(z(_Z_CKV, _Z_QI), gkv_ref[...]).astype(_BF16)
    qi_ref[...] = z(_Z_QI, _Z_POOL).astype(_BF16)
    pool_ref[...] = z(_Z_POOL, _Z_GA)
    sga_ref[...] = jax.nn.sigmoid(z(_Z_GA, _Z_GB)).astype(_BF16)
    sgb_ref[...] = jax.nn.sigmoid(z(_Z_GB, _Z_TAIL)).astype(_BF16)
    tail = z(_Z_TAIL, _Z_TOTAL)
    ki_ref[...] = tail[:, :IDX_DIM].astype(_BF16)
    tail_t = tail.T
    wt_ref[0] = tail_t[IDX_DIM:IDX_DIM + IDX_HEADS, :] * IDX_SCALE


def _proj(x, g, w, gkv, wuk, batch, seq):
    n = x.shape[0]
    tps = seq // TOKEN_TILE

    def flat(width):
        return pl.BlockSpec((TOKEN_TILE, width), lambda i: (i, 0))

    out_shape = [
        jax.ShapeDtypeStruct((batch, N_HEADS, seq, KV_LATENT), _BF16),
        jax.ShapeDtypeStruct((n, KV_LATENT), _BF16),
        jax.ShapeDtypeStruct((n, IDX_HEADS * IDX_DIM), _BF16),
        jax.ShapeDtypeStruct((n, IDX_DIM), _BF16),
        jax.ShapeDtypeStruct((batch, IDX_HEADS, seq), _F32),
        jax.ShapeDtypeStruct((n, POOL_WIDTH), _F32),
        jax.ShapeDtypeStruct((n, D_MODEL), _BF16),
        jax.ShapeDtypeStruct((n, D_MODEL), _BF16),
    ]
    out_specs = [
        pl.BlockSpec((1, N_HEADS, TOKEN_TILE, KV_LATENT), lambda i: (i // tps, 0, i % tps, 0)),
        flat(KV_LATENT), flat(IDX_HEADS * IDX_DIM), flat(IDX_DIM),
        pl.BlockSpec((1, IDX_HEADS, TOKEN_TILE), lambda i: (i // tps, 0, i % tps)),
        flat(POOL_WIDTH), flat(D_MODEL), flat(D_MODEL),
    ]
    return pl.pallas_call(
        _proj_kernel,
        grid=(n // TOKEN_TILE,),
        in_specs=[flat(D_MODEL), _const_spec((1, D_MODEL)), _const_spec((D_MODEL, _Z_TOTAL)),
                  _const_spec((1, KV_LATENT)), _const_spec((N_HEADS, HEAD_DIM, KV_LATENT))],
        out_specs=out_specs,
        out_shape=out_shape,
        compiler_params=pltpu.CompilerParams(dimension_semantics=("arbitrary",), vmem_limit_bytes=VMEM_LIMIT_BYTES),
        name="mix_proj",
    )(x, g, w, gkv, wuk)


def _attn_kernel(ql_ref, qi_ref, wt_ref, ki_ref, ckv_ref, wuv_ref, o_ref, keys_ref, *, k_sel):
    qb = pl.program_id(1)
    n_chunks = qb + 1
    q_pos = qb * Q_TILE + lax.broadcasted_iota(jnp.int32, (1, Q_TILE), 1)
    k_eff = jnp.minimum(k_sel, q_pos + 1)

    def score_chunk(c, _):
        k0 = pl.multiple_of(c * K_CHUNK, K_CHUNK)
        kc = ki_ref[0, pl.ds(k0, K_CHUNK), :]
        s = jnp.zeros((K_CHUNK, Q_TILE), _F32)
        for j in range(IDX_HEADS):
            r = _dot_nt(kc, qi_ref[0, :, j * IDX_DIM:(j + 1) * IDX_DIM])
            s = s + jnp.maximum(r, 0.0) * wt_ref[0, j:j + 1, :]
        bits = pltpu.bitcast(s, jnp.int32)
        key = bits ^ ((bits >> 31) & 0x7FFFFFFF)
        k_pos = k0 + lax.broadcasted_iota(jnp.int32, (K_CHUNK, Q_TILE), 0)
        keys_ref[pl.ds(k0, K_CHUNK), :] = jnp.where(k_pos <= q_pos, key, _INT_MIN)
        return 0

    lax.fori_loop(0, n_chunks, score_chunk, 0)

    def count_ge(cand):
        def body(c, acc):
            k0 = pl.multiple_of(c * K_CHUNK, K_CHUNK)
            hit = (keys_ref[pl.ds(k0, K_CHUNK), :] >= cand).astype(jnp.int32)
            return acc + jnp.sum(hit.reshape(K_CHUNK // 8, 8, Q_TILE), axis=0)

        acc = lax.fori_loop(0, n_chunks, body, jnp.zeros((8, Q_TILE), jnp.int32))
        return jnp.sum(acc, axis=0, keepdims=True)

    zero = jnp.zeros((1, Q_TILE), jnp.int32)
    thr = jnp.where(count_ge(zero) >= k_eff, zero, _INT_MIN)

    def bisect(it, thr):
        cand = thr + jnp.left_shift(jnp.int32(1), 30 - it)
        return jnp.where(count_ge(cand) >= k_eff, cand, thr)

    thr = lax.fori_loop(0, 31, bisect, thr)

    def count_gt(c, acc):
        k0 = pl.multiple_of(c * K_CHUNK, K_CHUNK)
        hit = (keys_ref[pl.ds(k0, K_CHUNK), :] > thr).astype(jnp.int32)
        return acc + jnp.sum(hit.reshape(K_CHUNK // 8, 8, Q_TILE), axis=0)

    n_gt = jnp.sum(lax.fori_loop(0, n_chunks, count_gt, jnp.zeros((8, Q_TILE), jnp.int32)), axis=0, keepdims=True)
    need = (k_eff - n_gt).astype(_F32)

    row = lax.broadcasted_iota(jnp.int32, (K_CHUNK, K_CHUNK), 0)
    col = lax.broadcasted_iota(jnp.int32, (K_CHUNK, K_CHUNK), 1)
    tri = (col <= row).astype(_BF16)

    q_lat = ql_ref[0].reshape(N_HEADS * Q_TILE, KV_LATENT)

    def attend(c, carry):
        m_old, l_old, acc, seen = carry
        k0 = pl.multiple_of(c * K_CHUNK, K_CHUNK)
        key = keys_ref[pl.ds(k0, K_CHUNK), :]
        eq = key == thr
        rank = seen + _dot(tri, eq.astype(_BF16))
        sel_t = (key > thr) | (eq & (rank <= need))
        seen = seen + jnp.sum(eq.astype(_F32), axis=0, keepdims=True)
        sel = sel_t.astype(_F32).T > 0.5
        ckv = ckv_ref[0, pl.ds(k0, K_CHUNK), :]
        logit = _dot_nt(q_lat, ckv).reshape(N_HEADS, Q_TILE, K_CHUNK)
        masked = jnp.where(sel[None], logit, _NEG_BIG)
        m_new = jnp.maximum(m_old, jnp.max(masked, axis=-1, keepdims=True))
        p = jnp.where(sel[None], jnp.exp(logit - m_new), 0.0)
        alpha = jnp.exp(m_old - m_new)
        l_new = alpha * l_old + jnp.sum(p, axis=-1, keepdims=True)
        pv = _dot(p.reshape(N_HEADS * Q_TILE, K_CHUNK).astype(_BF16), ckv)
        acc = alpha * acc + pv.reshape(N_HEADS, Q_TILE, KV_LATENT)
        return m_new, l_new, acc, seen

    init = (jnp.full((N_HEADS, Q_TILE, 1), _NEG_BIG, _F32), jnp.zeros((N_HEADS, Q_TILE, 1), _F32),
            jnp.zeros((N_HEADS, Q_TILE, KV_LATENT), _F32), jnp.zeros((1, Q_TILE), _F32))
    _, l_fin, acc, _ = lax.fori_loop(0, n_chunks, attend, init)
    o_lat = (acc / l_fin).astype(_BF16)
    out = [_dot(o_lat[hd], wuv_ref[hd]) for hd in range(N_HEADS)]
    o_ref[0] = jnp.concatenate(out, axis=-1).astype(_BF16)


def _attn(ql, qi, wt, ki, ckv, wuv, k_sel):
    batch, _, seq, _ = ql.shape
    return pl.pallas_call(
        functools.partial(_attn_kernel, k_sel=k_sel),
        grid=(batch, seq // Q_TILE),
        in_specs=[
            pl.BlockSpec((1, N_HEADS, Q_TILE, KV_LATENT), lambda b, i: (b, 0, i, 0)),
            pl.BlockSpec((1, Q_TILE, IDX_HEADS * IDX_DIM), lambda b, i: (b, i, 0)),
            pl.BlockSpec((1, IDX_HEADS, Q_TILE), lambda b, i: (b, 0, i)),
            pl.BlockSpec((1, seq, IDX_DIM), lambda b, i: (b, 0, 0)),
            pl.BlockSpec((1, seq, KV_LATENT), lambda b, i: (b, 0, 0)),
            _const_spec((N_HEADS, KV_LATENT, HEAD_DIM)),
        ],
        out_specs=pl.BlockSpec((1, Q_TILE, N_HEADS * HEAD_DIM), lambda b, i: (b, i, 0)),
        out_shape=jax.ShapeDtypeStruct((batch, seq, N_HEADS * HEAD_DIM), _BF16),
        scratch_shapes=[pltpu.VMEM((seq, Q_TILE), jnp.int32)],
        compiler_params=pltpu.CompilerParams(dimension_semantics=("arbitrary", "arbitrary"),
                                             vmem_limit_bytes=VMEM_LIMIT_BYTES),
        name="dsa_attn",
    )(ql, qi, wt, ki, ckv, wuv)


def _merge_kernel(x_ref, attn_ref, pool_ref, halo_ref, sga_ref, sgb_ref, wba_ref, pw_ref, ps_ref, wbp_ref, wo_ref,
                  o_ref, *, tiles_per_seq):
    tile = pl.program_id(0) % tiles_per_seq
    cur = pool_ref[...]
    halo = jnp.where(tile == 0, 0.0, halo_ref[...])
    ext = jnp.concatenate([halo, cur], axis=0)
    t1 = (tile * TOKEN_TILE + 1 + lax.broadcasted_iota(jnp.int32, (TOKEN_TILE, 1), 0)).astype(_F32)
    mixed = []
    win_sum, width = ext, 1
    for g, w in enumerate(POOL_WINDOWS):
        while width < w:
            win_sum = win_sum + pltpu.roll(win_sum, width, 0)
            width *= 2
        cols = slice(g * POOL_GROUP, (g + 1) * POOL_GROUP)
        pooled = win_sum[POOL_HALO:, cols] / jnp.minimum(t1, float(w)) - cur[:, cols]
        mixed.append(_dot(pooled.astype(_BF16), pw_ref[g]))
    mixed = (jnp.concatenate(mixed, axis=-1) * ps_ref[...]).astype(_BF16)
    merged = (sga_ref[...].astype(_F32) * _dot(attn_ref[...], wba_ref[...])
              + sgb_ref[...].astype(_F32) * _dot(mixed, wbp_ref[...]))
    o_ref[...] = x_ref[...] + _dot(merged.astype(_BF16), wo_ref[...])


def _merge(x, attn, pool, sga, sgb, wba, pw, ps, wbp, wo, seq):
    n = x.shape[0]
    tps = seq // TOKEN_TILE
    halo_blocks = TOKEN_TILE // POOL_HALO

    def flat(width):
        return pl.BlockSpec((TOKEN_TILE, width), lambda i: (i, 0))

    return pl.pallas_call(
        functools.partial(_merge_kernel, tiles_per_seq=tps),
        grid=(n // TOKEN_TILE,),
        in_specs=[
            flat(D_MODEL), flat(N_HEADS * HEAD_DIM), flat(POOL_WIDTH),
            pl.BlockSpec((POOL_HALO, POOL_WIDTH), lambda i: (jnp.maximum(i * halo_blocks - 1, 0), 0)),
            flat(D_MODEL), flat(D_MODEL),
            _const_spec((N_HEADS * HEAD_DIM, D_MODEL)), _const_spec((len(POOL_WINDOWS), POOL_GROUP, POOL_GROUP)),
            _const_spec((1, POOL_WIDTH)), _const_spec((POOL_WIDTH, D_MODEL)), _const_spec((D_MODEL, D_MODEL)),
        ],
        out_specs=flat(D_MODEL),
        out_shape=jax.ShapeDtypeStruct((n, D_MODEL), _F32),
        compiler_params=pltpu.CompilerParams(dimension_semantics=("arbitrary",), vmem_limit_bytes=VMEM_LIMIT_BYTES),
        name="merge_out",
    )(x, attn, pool, pool, sga, sgb, wba, pw, ps, wbp, wo)


def _reorder_w_in(w):
    cuts, off = {}, 0
    for name, width in (("q", N_HEADS * HEAD_DIM), ("ckv", KV_LATENT), ("qi", IDX_HEADS * IDX_DIM), ("ki", IDX_DIM),
                        ("wi", IDX_HEADS), ("pool", POOL_WIDTH), ("ga", D_MODEL), ("gb", D_MODEL)):
        cuts[name] = w[:, off:off + width]
        off += width
    pad = jnp.zeros((w.shape[0], LANES - IDX_DIM - IDX_HEADS), w.dtype)
    cols = [cuts[k] for k in ("q", "ckv", "qi", "pool", "ga", "gb", "ki", "wi")] + [pad]
    return jnp.concatenate(cols, axis=1).astype(_BF16)


def kernel(x, norm_ffn1, ffn1_gate, ffn1_up, ffn1_down, norm_mix, w_in, norm_kv, w_uk, w_uv, pool_w, pool_scale,
           w_branch_attn, w_branch_pool, w_out, norm_ffn2, ffn2_gate, ffn2_up, ffn2_down, norm_final):
    batch, seq, _ = x.shape
    depth = norm_ffn1.shape[0]
    assert seq % TOKEN_TILE == 0 and seq % Q_TILE == 0
    k_sel = min(TOPK_MAX, seq // 4)
    n = batch * seq
    bf = lambda a: a.astype(_BF16)
    row = lambda a: a.reshape(1, -1)
    gf = row(norm_final)
    h = x.reshape(n, D_MODEL)
    for i in range(depth):
        h = _ffn(h, row(norm_ffn1[i]), bf(ffn1_gate[i]), bf(ffn1_up[i]), bf(ffn1_down[i]), gf, False)
        wuk_t = bf(jnp.transpose(w_uk[i], (1, 2, 0)))
        wuv_t = bf(jnp.transpose(w_uv[i], (1, 0, 2)))
        ql, ckv, qi, ki, wt, pool, sga, sgb = _proj(h, row(norm_mix[i]), _reorder_w_in(w_in[i]), row(norm_kv[i]),
                                                    wuk_t, batch, seq)
        attn = _attn(ql, qi.reshape(batch, seq, -1), wt, ki.reshape(batch, seq, -1), ckv.reshape(batch, seq, -1),
                     wuv_t, k_sel)
        h = _merge(h, attn.reshape(n, -1), pool, sga, sgb, bf(w_branch_attn[i]), bf(pool_w[i]), row(pool_scale[i]),
                   bf(w_branch_pool[i]), bf(w_out[i]), seq)
        h = _ffn(h, row(norm_ffn2[i]), bf(ffn2_gate[i]), bf(ffn2_up[i]), bf(ffn2_down[i]), gf, i == depth - 1)
    return h.reshape(batch, seq, D_MODEL)
```

```python
import functools

import jax
import jax.numpy as jnp
from jax import lax
from jax.experimental import pallas as pl
from jax.experimental.pallas import tpu as pltpu

D_MODEL = 1024
N_HEADS = 8
HEAD_DIM = 64
KV_LATENT = 128
ATTN_SCALE = HEAD_DIM ** -0.5
IDX_HEADS = 8
IDX_DIM = 64
IDX_SCALE = (IDX_HEADS ** -0.5) * (IDX_DIM ** -0.5)
TOPK_MAX = 256
POOL_WINDOWS = (2, 4, 8, 16)
POOL_GROUP = 128
POOL_WIDTH = POOL_GROUP * len(POOL_WINDOWS)
POOL_HALO = 16
D_FF = 2816
EPS = 1e-6

LANES = 128
BF16_SUBLANES = 16
VMEM_LIMIT_BYTES = 56 * 1024 * 1024

TOKEN_TILE = 512
FF_CHUNK = 256
Q_TILE = 256
K_CHUNK = Q_TILE
COUNT_ROWS = 32

_N_CKV = 0
_N_POOL = _N_CKV + KV_LATENT
_N_GA = _N_POOL + POOL_WIDTH
_N_GB = _N_GA + D_MODEL
_N_KI = _N_GB + D_MODEL
_N_TOTAL = _N_KI + LANES
_T_Q = 0
_T_QI = _T_Q + N_HEADS * HEAD_DIM
_T_WI = _T_QI + IDX_HEADS * IDX_DIM
_T_TOTAL = _T_WI + BF16_SUBLANES

_F32 = jnp.float32
_BF16 = jnp.bfloat16
_INT_MIN = -(2 ** 31)
_NEG_BIG = float(jnp.finfo(jnp.float32).min)


def _const_spec(shape):
    return pl.BlockSpec(shape, lambda *_: (0,) * len(shape), pipeline_mode=pl.Buffered(1))


def _rms(x, g):
    return x * lax.rsqrt(jnp.mean(x * x, axis=-1, keepdims=True) + EPS) * g


def _dot(a, b):
    return jnp.dot(a, b, preferred_element_type=_F32)


def _ffn_kernel(x_ref, g_ref, wg_ref, wu_ref, wd_ref, gf_ref, o_ref, *, final_norm):
    x = x_ref[...]
    h = _rms(x, g_ref[...]).astype(_BF16)
    acc = jnp.zeros(x.shape, _F32)
    for c in range(D_FF // FF_CHUNK):
        sl = slice(c * FF_CHUNK, (c + 1) * FF_CHUNK)
        gate = _dot(h, wg_ref[:, sl])
        up = _dot(h, wu_ref[:, sl])
        act = (gate * jax.nn.sigmoid(gate) * up).astype(_BF16)
        acc = acc + _dot(act, wd_ref[sl, :])
    y = x + 0.5 * acc
    if final_norm:
        y = _rms(y, gf_ref[...])
    o_ref[...] = y


def _ffn(x, g, wg, wu, wd, gf, final_norm):
    n = x.shape[0]
    tile = pl.BlockSpec((TOKEN_TILE, D_MODEL), lambda i: (i, 0))
    return pl.pallas_call(
        functools.partial(_ffn_kernel, final_norm=final_norm),
        grid=(n // TOKEN_TILE,),
        in_specs=[tile, _const_spec((1, D_MODEL)), _const_spec((D_MODEL, D_FF)), _const_spec((D_MODEL, D_FF)),
                  _const_spec((D_FF, D_MODEL)), _const_spec((1, D_MODEL))],
        out_specs=tile,
        out_shape=jax.ShapeDtypeStruct((n, D_MODEL), _F32),
        compiler_params=pltpu.CompilerParams(dimension_semantics=("arbitrary",), vmem_limit_bytes=VMEM_LIMIT_BYTES),
        name="ffn",
    )(x, g, wg, wu, wd, gf)


def _proj_kernel(x_ref, g_ref, wn_ref, wt_ref, gkv_ref, wuk_ref, qlt_ref, ckv_ref, ckvt_ref, qit_ref, ki_ref, wit_ref,
                 pool_ref, sga_ref, sgb_ref):
    h32 = _rms(x_ref[...], g_ref[...])
    h = h32.astype(_BF16)
    ht = h32.T.astype(_BF16)

    def z(lo, hi):
        return _dot(h, wn_ref[:, lo:hi])

    def zt(lo, hi):
        return _dot(wt_ref[lo:hi, :], ht)

    ckv = _rms(z(_N_CKV, _N_POOL), gkv_ref[...])
    ckv_ref[...] = ckv.astype(_BF16)
    ckvt_ref[0] = ckv.T.astype(_BF16)
    pool_ref[...] = z(_N_POOL, _N_GA)
    sga_ref[...] = jax.nn.sigmoid(z(_N_GA, _N_GB)).astype(_BF16)
    sgb_ref[...] = jax.nn.sigmoid(z(_N_GB, _N_KI)).astype(_BF16)
    ki_ref[...] = z(_N_KI, _N_TOTAL)[:, :IDX_DIM].astype(_BF16)

    qt = zt(_T_Q, _T_QI).astype(_BF16)
    for hd in range(N_HEADS):
        qlt = _dot(wuk_ref[hd], qt[hd * HEAD_DIM:(hd + 1) * HEAD_DIM, :])
        qlt_ref[0, hd] = (qlt * ATTN_SCALE).astype(_BF16)
    qit_ref[0] = zt(_T_QI, _T_WI).astype(_BF16)
    wit_ref[0] = zt(_T_WI, _T_TOTAL)[:IDX_HEADS, :] * IDX_SCALE


def _proj(x, g, wn, wt, gkv, wuk, batch, seq):
    n = x.shape[0]
    tps = seq // TOKEN_TILE

    def flat(width):
        return pl.BlockSpec((TOKEN_TILE, width), lambda i: (i, 0))

    def feat(rows):
        return pl.BlockSpec((1, rows, TOKEN_TILE), lambda i: (i // tps, 0, i % tps))

    out_shape = [
        jax.ShapeDtypeStruct((batch, N_HEADS, KV_LATENT, seq), _BF16),
        jax.ShapeDtypeStruct((n, KV_LATENT), _BF16),
        jax.ShapeDtypeStruct((batch, KV_LATENT, seq), _BF16),
        jax.ShapeDtypeStruct((batch, IDX_HEADS * IDX_DIM, seq), _BF16),
        jax.ShapeDtypeStruct((n, IDX_DIM), _BF16),
        jax.ShapeDtypeStruct((batch, IDX_HEADS, seq), _F32),
        jax.ShapeDtypeStruct((n, POOL_WIDTH), _F32),
        jax.ShapeDtypeStruct((n, D_MODEL), _BF16),
        jax.ShapeDtypeStruct((n, D_MODEL), _BF16),
    ]
    out_specs = [
        pl.BlockSpec((1, N_HEADS, KV_LATENT, TOKEN_TILE), lambda i: (i // tps, 0, 0, i % tps)),
        flat(KV_LATENT), feat(KV_LATENT), feat(IDX_HEADS * IDX_DIM), flat(IDX_DIM), feat(IDX_HEADS),
        flat(POOL_WIDTH), flat(D_MODEL), flat(D_MODEL),
    ]
    return pl.pallas_call(
        _proj_kernel,
        grid=(n // TOKEN_TILE,),
        in_specs=[flat(D_MODEL), _const_spec((1, D_MODEL)), _const_spec((D_MODEL, _N_TOTAL)),
                  _const_spec((_T_TOTAL, D_MODEL)), _const_spec((1, KV_LATENT)),
                  _const_spec((N_HEADS, KV_LATENT, HEAD_DIM))],
        out_specs=out_specs,
        out_shape=out_shape,
        compiler_params=pltpu.CompilerParams(dimension_semantics=("arbitrary",), vmem_limit_bytes=VMEM_LIMIT_BYTES),
        name="mix_proj",
    )(x, g, wn, wt, gkv, wuk)


def _attn_kernel(qlt_ref, qit_ref, wit_ref, ki_ref, ckv_ref, ckvt_ref, wuvt_ref, o_ref, keys_ref, m_ref, l_ref, acc_ref,
                 *, k_sel):
    qb = pl.program_id(1)
    n_chunks = qb + 1
    q_pos = qb * Q_TILE + lax.broadcasted_iota(jnp.int32, (1, Q_TILE), 1)
    k_eff = jnp.minimum(k_sel, q_pos + 1)
    row_iota = lax.broadcasted_iota(jnp.int32, (K_CHUNK, Q_TILE), 0)

    def chunk_start(c):
        return pl.multiple_of(c * K_CHUNK, K_CHUNK)

    def score_chunk(c, _):
        k0 = chunk_start(c)
        kc = ki_ref[0, pl.ds(k0, K_CHUNK), :]
        s = jnp.zeros((K_CHUNK, Q_TILE), _F32)
        for j in range(IDX_HEADS):
            r = _dot(kc, qit_ref[0, j * IDX_DIM:(j + 1) * IDX_DIM, :])
            s = s + jnp.maximum(r, 0.0) * wit_ref[0, j:j + 1, :]
        bits = pltpu.bitcast(s, jnp.int32)
        key = bits ^ ((bits >> 31) & 0x7FFFFFFF)
        keys_ref[pl.ds(k0, K_CHUNK), :] = jnp.where(k0 + row_iota <= q_pos, key, _INT_MIN)
        return 0

    lax.fori_loop(0, n_chunks, score_chunk, 0)

    def count(pred):
        def body(c, acc):
            k0 = chunk_start(c)
            hit = pred(keys_ref[pl.ds(k0, K_CHUNK), :], k0).astype(jnp.int32)
            return acc + jnp.sum(hit.reshape(K_CHUNK // COUNT_ROWS, COUNT_ROWS, Q_TILE), axis=0)

        acc = lax.fori_loop(0, n_chunks, body, jnp.zeros((COUNT_ROWS, Q_TILE), jnp.int32))
        return jnp.sum(acc, axis=0, keepdims=True)

    zero = jnp.zeros((1, Q_TILE), jnp.int32)
    n_nonneg = count(lambda k, _: k >= zero)
    sign_ok = n_nonneg >= k_eff
    thr = jnp.where(sign_ok, zero, _INT_MIN)
    n_ge = jnp.where(sign_ok, n_nonneg, (q_pos + 1))

    def bisect(it, carry):
        thr, n_ge = carry
        cand = thr + jnp.left_shift(jnp.int32(1), 30 - it)
        n_cand = count(lambda k, _: k >= cand)
        ok = n_cand >= k_eff
        return jnp.where(ok, cand, thr), jnp.where(ok, n_cand, n_ge)

    thr, n_ge = lax.fori_loop(0, 31, bisect, (thr, n_ge))

    @pl.when(jnp.max(n_ge - k_eff) > 0)
    def _():
        need = k_eff - count(lambda k, _: k > thr)

        def ties_before(limit):
            return count(lambda k, k0: (k == thr) & (k0 + row_iota < limit))

        def grow(it, limit):
            cand = limit + jnp.left_shift(jnp.int32(1), 12 - it)
            return jnp.where(ties_before(cand) <= need, cand, limit)

        limit = lax.fori_loop(0, 13, grow, zero)

        def demote(c, _):
            k0 = chunk_start(c)
            k = keys_ref[pl.ds(k0, K_CHUNK), :]
            keys_ref[pl.ds(k0, K_CHUNK), :] = jnp.where((k == thr) & (k0 + row_iota >= limit), thr - 1, k)
            return 0

        lax.fori_loop(0, n_chunks, demote, 0)

    m_ref[...] = jnp.full(m_ref.shape, _NEG_BIG, _F32)
    l_ref[...] = jnp.zeros(l_ref.shape, _F32)
    acc_ref[...] = jnp.zeros(acc_ref.shape, _F32)

    def attend(c, _):
        k0 = chunk_start(c)
        sel = keys_ref[pl.ds(k0, K_CHUNK), :] >= thr
        ckv = ckv_ref[0, pl.ds(k0, K_CHUNK), :]
        ckvt = ckvt_ref[0, :, pl.ds(k0, K_CHUNK)]
        for hd in range(N_HEADS):
            logit = jnp.where(sel, _dot(ckv, qlt_ref[0, hd]), _NEG_BIG)
            m_old = m_ref[hd:hd + 1, :]
            m_new = jnp.maximum(m_old, jnp.max(logit, axis=0, keepdims=True))
            p = jnp.exp(logit - m_new)
            alpha = jnp.exp(m_old - m_new)
            l_ref[hd:hd + 1, :] = alpha * l_ref[hd:hd + 1, :] + jnp.sum(p, axis=0, keepdims=True)
            acc_ref[hd] = alpha * acc_ref[hd] + _dot(ckvt, p.astype(_BF16))
            m_ref[hd:hd + 1, :] = m_new
        return 0

    lax.fori_loop(0, n_chunks, attend, 0)

    out_t = []
    for hd in range(N_HEADS):
        o_lat_t = (acc_ref[hd] / l_ref[hd:hd + 1, :]).astype(_BF16)
        out_t.append(_dot(wuvt_ref[hd], o_lat_t))
    o_ref[0] = jnp.concatenate(out_t, axis=0).T.astype(_BF16)


def _attn(qlt, qit, wit, ki, ckv, ckvt, wuvt, k_sel):
    batch, _, _, seq = qlt.shape
    return pl.pallas_call(
        functools.partial(_attn_kernel, k_sel=k_sel),
        grid=(batch, seq // Q_TILE),
        in_specs=[
            pl.BlockSpec((1, N_HEADS, KV_LATENT, Q_TILE), lambda b, i: (b, 0, 0, i)),
            pl.BlockSpec((1, IDX_HEADS * IDX_DIM, Q_TILE), lambda b, i: (b, 0, i)),
            pl.BlockSpec((1, IDX_HEADS, Q_TILE), lambda b, i: (b, 0, i)),
            pl.BlockSpec((1, seq, IDX_DIM), lambda b, i: (b, 0, 0)),
            pl.BlockSpec((1, seq, KV_LATENT), lambda b, i: (b, 0, 0)),
            pl.BlockSpec((1, KV_LATENT, seq), lambda b, i: (b, 0, 0)),
            _const_spec((N_HEADS, HEAD_DIM, KV_LATENT)),
        ],
        out_specs=pl.BlockSpec((1, Q_TILE, N_HEADS * HEAD_DIM), lambda b, i: (b, i, 0)),
        out_shape=jax.ShapeDtypeStruct((batch, seq, N_HEADS * HEAD_DIM), _BF16),
        scratch_shapes=[pltpu.VMEM((seq, Q_TILE), jnp.int32), pltpu.VMEM((N_HEADS, Q_TILE), _F32),
                        pltpu.VMEM((N_HEADS, Q_TILE), _F32), pltpu.VMEM((N_HEADS, KV_LATENT, Q_TILE), _F32)],
        compiler_params=pltpu.CompilerParams(dimension_semantics=("arbitrary", "arbitrary"),
                                             vmem_limit_bytes=VMEM_LIMIT_BYTES),
        name="dsa_attn",
    )(qlt, qit, wit, ki, ckv, ckvt, wuvt)


def _merge_kernel(x_ref, attn_ref, pool_ref, halo_ref, sga_ref, sgb_ref, wba_ref, pw_ref, ps_ref, wbp_ref, wo_ref,
                  o_ref, *, tiles_per_seq):
    tile = pl.program_id(0) % tiles_per_seq
    cur = pool_ref[...]
    halo = jnp.where(tile == 0, 0.0, halo_ref[...])
    ext = jnp.concatenate([halo, cur], axis=0)
    t1 = (tile * TOKEN_TILE + 1 + lax.broadcasted_iota(jnp.int32, (TOKEN_TILE, 1), 0)).astype(_F32)
    mixed = []
    win_sum, width = ext, 1
    for g, w in enumerate(POOL_WINDOWS):
        while width < w:
            win_sum = win_sum + pltpu.roll(win_sum, width, 0)
            width *= 2
        cols = slice(g * POOL_GROUP, (g + 1) * POOL_GROUP)
        pooled = win_sum[POOL_HALO:, cols] / jnp.minimum(t1, float(w)) - cur[:, cols]
        mixed.append(_dot(pooled.astype(_BF16), pw_ref[g]))
    mixed = (jnp.concatenate(mixed, axis=-1) * ps_ref[...]).astype(_BF16)
    merged = (sga_ref[...].astype(_F32) * _dot(attn_ref[...], wba_ref[...])
              + sgb_ref[...].astype(_F32) * _dot(mixed, wbp_ref[...]))
    o_ref[...] = x_ref[...] + _dot(merged.astype(_BF16), wo_ref[...])


def _merge(x, attn, pool, sga, sgb, wba, pw, ps, wbp, wo, seq):
    n = x.shape[0]
    tps = seq // TOKEN_TILE
    halo_blocks = TOKEN_TILE // POOL_HALO

    def flat(width):
        return pl.BlockSpec((TOKEN_TILE, width), lambda i: (i, 0))

    return pl.pallas_call(
        functools.partial(_merge_kernel, tiles_per_seq=tps),
        grid=(n // TOKEN_TILE,),
        in_specs=[
            flat(D_MODEL), flat(N_HEADS * HEAD_DIM), flat(POOL_WIDTH),
            pl.BlockSpec((POOL_HALO, POOL_WIDTH), lambda i: (jnp.maximum(i * halo_blocks - 1, 0), 0)),
            flat(D_MODEL), flat(D_MODEL),
            _const_spec((N_HEADS * HEAD_DIM, D_MODEL)), _const_spec((len(POOL_WINDOWS), POOL_GROUP, POOL_GROUP)),
            _const_spec((1, POOL_WIDTH)), _const_spec((POOL_WIDTH, D_MODEL)), _const_spec((D_MODEL, D_MODEL)),
        ],
        out_specs=flat(D_MODEL),
        out_shape=jax.ShapeDtypeStruct((n, D_MODEL), _F32),
        compiler_params=pltpu.CompilerParams(dimension_semantics=("arbitrary",), vmem_limit_bytes=VMEM_LIMIT_BYTES),
        name="merge_out",
    )(x, attn, pool, pool, sga, sgb, wba, pw, ps, wbp, wo)


def _split_w_in(w):
    cuts, off = {}, 0
    for name, width in (("q", N_HEADS * HEAD_DIM), ("ckv", KV_LATENT), ("qi", IDX_HEADS * IDX_DIM), ("ki", IDX_DIM),
                        ("wi", IDX_HEADS), ("pool", POOL_WIDTH), ("ga", D_MODEL), ("gb", D_MODEL)):
        cuts[name] = w[:, off:off + width]
        off += width
    pad_n = jnp.zeros((w.shape[0], LANES - IDX_DIM), w.dtype)
    wn = jnp.concatenate([cuts["ckv"], cuts["pool"], cuts["ga"], cuts["gb"], cuts["ki"], pad_n], axis=1)
    pad_t = jnp.zeros((w.shape[0], BF16_SUBLANES - IDX_HEADS), w.dtype)
    wt = jnp.concatenate([cuts["q"], cuts["qi"], cuts["wi"], pad_t], axis=1).T
    return wn.astype(_BF16), wt.astype(_BF16)


def kernel(x, norm_ffn1, ffn1_gate, ffn1_up, ffn1_down, norm_mix, w_in, norm_kv, w_uk, w_uv, pool_w, pool_scale,
           w_branch_attn, w_branch_pool, w_out, norm_ffn2, ffn2_gate, ffn2_up, ffn2_down, norm_final):
    batch, seq, _ = x.shape
    depth = norm_ffn1.shape[0]
    assert seq % TOKEN_TILE == 0 and seq % Q_TILE == 0
    k_sel = min(TOPK_MAX, seq // 4)
    n = batch * seq
    bf = lambda a: a.astype(_BF16)
    row = lambda a: a.reshape(1, -1)
    gf = row(norm_final)
    h = x.reshape(n, D_MODEL)
    for i in range(depth):
        h = _ffn(h, row(norm_ffn1[i]), bf(ffn1_gate[i]), bf(ffn1_up[i]), bf(ffn1_down[i]), gf, False)
        wn, wt = _split_w_in(w_in[i])
        wuk_h = bf(jnp.transpose(w_uk[i], (1, 0, 2)))
        wuvt_h = bf(jnp.transpose(w_uv[i], (1, 2, 0)))
        qlt, ckv, ckvt, qit, ki, wit, pool, sga, sgb = _proj(h, row(norm_mix[i]), wn, wt, row(norm_kv[i]), wuk_h,
                                                             batch, seq)
        attn = _attn(qlt, qit, wit, ki.reshape(batch, seq, -1), ckv.reshape(batch, seq, -1), ckvt, wuvt_h, k_sel)
        h = _merge(h, attn.reshape(n, -1), pool, sga, sgb, bf(w_branch_attn[i]), bf(pool_w[i]), row(pool_scale[i]),
                   bf(w_branch_pool[i]), bf(w_out[i]), seq)
        h = _ffn(h, row(norm_ffn2[i]), bf(ffn2_gate[i]), bf(ffn2_up[i]), bf(ffn2_down[i]), gf, i == depth - 1)
    return h.reshape(batch, seq, D_MODEL)
```

```python
import functools

import jax
import jax.numpy as jnp
from jax import lax
from jax.experimental import pallas as pl
from jax.experimental.pallas import tpu as pltpu

D_MODEL = 1024
N_HEADS = 8
HEAD_DIM = 64
KV_LATENT = 128
ATTN_SCALE = HEAD_DIM ** -0.5
LOG2_E = 1.4426950408889634
IDX_HEADS = 8
IDX_DIM = 64
IDX_SCALE = (IDX_HEADS ** -0.5) * (IDX_DIM ** -0.5)
TOPK_MAX = 256
POOL_WINDOWS = (2, 4, 8, 16)
POOL_GROUP = 128
POOL_WIDTH = POOL_GROUP * len(POOL_WINDOWS)
POOL_HALO = 16
D_FF = 2816
EPS = 1e-6

LANES = 128
BF16_SUBLANES = 16
KV_ROWS = KV_LATENT + BF16_SUBLANES
VMEM_LIMIT_BYTES = 56 * 1024 * 1024

TOKEN_TILE = 512
FF_CHUNK = 256
Q_TILE = 256
K_CHUNK = Q_TILE
COUNT_ROWS = 32

_N_CKV = 0
_N_POOL = _N_CKV + KV_LATENT
_N_GA = _N_POOL + POOL_WIDTH
_N_GB = _N_GA + D_MODEL
_N_KI = _N_GB + D_MODEL
_N_TOTAL = _N_KI + LANES
_T_Q = 0
_T_QI = _T_Q + N_HEADS * HEAD_DIM
_T_WI = _T_QI + IDX_HEADS * IDX_DIM
_T_TOTAL = _T_WI + BF16_SUBLANES

_F32 = jnp.float32
_BF16 = jnp.bfloat16
_INT_MIN = -(2 ** 31)
_NEG_BIG = float(jnp.finfo(jnp.float32).min)


def _const_spec(shape):
    return pl.BlockSpec(shape, lambda *_: (0,) * len(shape), pipeline_mode=pl.Buffered(1))


def _rms(x, g):
    return x * lax.rsqrt(jnp.mean(x * x, axis=-1, keepdims=True) + EPS) * g


def _dot(a, b):
    return jnp.dot(a, b, preferred_element_type=_F32)


def _ffn_kernel(x_ref, g_ref, wg_ref, wu_ref, wd_ref, gf_ref, o_ref, *, final_norm):
    x = x_ref[...]
    h = _rms(x, g_ref[...]).astype(_BF16)
    acc = jnp.zeros(x.shape, _F32)
    for c in range(D_FF // FF_CHUNK):
        sl = slice(c * FF_CHUNK, (c + 1) * FF_CHUNK)
        gate = _dot(h, wg_ref[:, sl])
        up = _dot(h, wu_ref[:, sl])
        act = (gate * jax.nn.sigmoid(gate) * up).astype(_BF16)
        acc = acc + _dot(act, wd_ref[sl, :])
    y = x + 0.5 * acc
    if final_norm:
        y = _rms(y, gf_ref[...])
    o_ref[...] = y


def _ffn(x, g, wg, wu, wd, gf, final_norm):
    n = x.shape[0]
    tile = pl.BlockSpec((TOKEN_TILE, D_MODEL), lambda i: (i, 0))
    return pl.pallas_call(
        functools.partial(_ffn_kernel, final_norm=final_norm),
        grid=(n // TOKEN_TILE,),
        in_specs=[tile, _const_spec((1, D_MODEL)), _const_spec((D_MODEL, D_FF)), _const_spec((D_MODEL, D_FF)),
                  _const_spec((D_FF, D_MODEL)), _const_spec((1, D_MODEL))],
        out_specs=tile,
        out_shape=jax.ShapeDtypeStruct((n, D_MODEL), _F32),
        compiler_params=pltpu.CompilerParams(dimension_semantics=("arbitrary",), vmem_limit_bytes=VMEM_LIMIT_BYTES),
        name="ffn",
    )(x, g, wg, wu, wd, gf)


def _proj_kernel(x_ref, g_ref, wn_ref, wt_ref, gkv_ref, wuk_ref, qlt_ref, ckv_ref, ckvt_ref, qit_ref, ki_ref, wit_ref,
                 pool_ref, sga_ref, sgb_ref):
    h32 = _rms(x_ref[...], g_ref[...])
    h = h32.astype(_BF16)
    ht = h32.T.astype(_BF16)

    def z(lo, hi):
        return _dot(h, wn_ref[:, lo:hi])

    def zt(lo, hi):
        return _dot(wt_ref[lo:hi, :], ht)

    ckv = _rms(z(_N_CKV, _N_POOL), gkv_ref[...])
    ckv_ref[...] = ckv.astype(_BF16)
    ones_tile = (lax.broadcasted_iota(jnp.int32, (BF16_SUBLANES, TOKEN_TILE), 0) == 0).astype(_F32)
    ckvt_ref[0] = jnp.concatenate([ckv.T, ones_tile], axis=0).astype(_BF16)
    pool_ref[...] = z(_N_POOL, _N_GA)
    sga_ref[...] = jax.nn.sigmoid(z(_N_GA, _N_GB)).astype(_BF16)
    sgb_ref[...] = jax.nn.sigmoid(z(_N_GB, _N_KI)).astype(_BF16)
    ki_ref[...] = z(_N_KI, _N_TOTAL)[:, :IDX_DIM].astype(_BF16)

    qt = zt(_T_Q, _T_QI).astype(_BF16)
    for hd in range(N_HEADS):
        qlt = (_dot(wuk_ref[hd], qt[hd * HEAD_DIM:(hd + 1) * HEAD_DIM, :]) * (ATTN_SCALE * LOG2_E)).astype(_BF16)
        for j in range(TOKEN_TILE // Q_TILE):
            qlt_ref[0, j, :, hd * Q_TILE:(hd + 1) * Q_TILE] = qlt[:, j * Q_TILE:(j + 1) * Q_TILE]
    qit_ref[0] = zt(_T_QI, _T_WI).astype(_BF16)
    wit_ref[0] = zt(_T_WI, _T_TOTAL)[:IDX_HEADS, :] * IDX_SCALE


def _proj(x, g, wn, wt, gkv, wuk, batch, seq):
    n = x.shape[0]
    tps = seq // TOKEN_TILE

    def flat(width):
        return pl.BlockSpec((TOKEN_TILE, width), lambda i: (i, 0))

    def feat(rows):
        return pl.BlockSpec((1, rows, TOKEN_TILE), lambda i: (i // tps, 0, i % tps))

    out_shape = [
        jax.ShapeDtypeStruct((batch, seq // Q_TILE, KV_LATENT, N_HEADS * Q_TILE), _BF16),
        jax.ShapeDtypeStruct((n, KV_LATENT), _BF16),
        jax.ShapeDtypeStruct((batch, KV_ROWS, seq), _BF16),
        jax.ShapeDtypeStruct((batch, IDX_HEADS * IDX_DIM, seq), _BF16),
        jax.ShapeDtypeStruct((n, IDX_DIM), _BF16),
        jax.ShapeDtypeStruct((batch, IDX_HEADS, seq), _F32),
        jax.ShapeDtypeStruct((n, POOL_WIDTH), _F32),
        jax.ShapeDtypeStruct((n, D_MODEL), _BF16),
        jax.ShapeDtypeStruct((n, D_MODEL), _BF16),
    ]
    out_specs = [
        pl.BlockSpec((1, TOKEN_TILE // Q_TILE, KV_LATENT, N_HEADS * Q_TILE), lambda i: (i // tps, i % tps, 0, 0)),
        flat(KV_LATENT), feat(KV_ROWS), feat(IDX_HEADS * IDX_DIM), flat(IDX_DIM), feat(IDX_HEADS),
        flat(POOL_WIDTH), flat(D_MODEL), flat(D_MODEL),
    ]
    return pl.pallas_call(
        _proj_kernel,
        grid=(n // TOKEN_TILE,),
        in_specs=[flat(D_MODEL), _const_spec((1, D_MODEL)), _const_spec((D_MODEL, _N_TOTAL)),
                  _const_spec((_T_TOTAL, D_MODEL)), _const_spec((1, KV_LATENT)),
                  _const_spec((N_HEADS, KV_LATENT, HEAD_DIM))],
        out_specs=out_specs,
        out_shape=out_shape,
        compiler_params=pltpu.CompilerParams(dimension_semantics=("arbitrary",), vmem_limit_bytes=VMEM_LIMIT_BYTES),
        name="mix_proj",
    )(x, g, wn, wt, gkv, wuk)


def _attn_kernel(qlt_ref, qit_ref, wit_ref, ki_ref, ckv_ref, ckvt_ref, wuvt_ref, o_ref, keys_ref, m_ref, acc_ref, *,
                 k_sel):
    qb = pl.program_id(1)
    n_chunks = qb + 1
    q_pos = qb * Q_TILE + lax.broadcasted_iota(jnp.int32, (1, Q_TILE), 1)
    k_eff = jnp.minimum(k_sel, q_pos + 1)
    row_iota = lax.broadcasted_iota(jnp.int32, (K_CHUNK, Q_TILE), 0)

    def chunk_start(c):
        return pl.multiple_of(c * K_CHUNK, K_CHUNK)

    def score_chunk(c, _):
        k0 = chunk_start(c)
        kc = ki_ref[0, pl.ds(k0, K_CHUNK), :]
        s = jnp.zeros((K_CHUNK, Q_TILE), _F32)
        for j in range(IDX_HEADS):
            r = _dot(kc, qit_ref[0, j * IDX_DIM:(j + 1) * IDX_DIM, :])
            s = s + jnp.maximum(r, 0.0) * wit_ref[0, j:j + 1, :]
        bits = pltpu.bitcast(s, jnp.int32)
        key = bits ^ ((bits >> 31) & 0x7FFFFFFF)
        keys_ref[pl.ds(k0, K_CHUNK), :] = jnp.where(k0 + row_iota <= q_pos, key, _INT_MIN)
        return 0

    lax.fori_loop(0, n_chunks, score_chunk, 0)

    def count(pred):
        def body(c, acc):
            k0 = chunk_start(c)
            hit = pred(keys_ref[pl.ds(k0, K_CHUNK), :], k0).astype(jnp.int32)
            return acc + jnp.sum(hit.reshape(K_CHUNK // COUNT_ROWS, COUNT_ROWS, Q_TILE), axis=0)

        acc = lax.fori_loop(0, n_chunks, body, jnp.zeros((COUNT_ROWS, Q_TILE), jnp.int32))
        return jnp.sum(acc, axis=0, keepdims=True)

    zero = jnp.zeros((1, Q_TILE), jnp.int32)
    n_nonneg = count(lambda k, _: k >= zero)
    sign_ok = n_nonneg >= k_eff
    thr = jnp.where(sign_ok, zero, _INT_MIN)
    n_ge = jnp.where(sign_ok, n_nonneg, (q_pos + 1))

    def bisect(it, carry):
        thr, n_ge = carry
        cand = thr + jnp.left_shift(jnp.int32(1), 30 - it)
        n_cand = count(lambda k, _: k >= cand)
        ok = n_cand >= k_eff
        return jnp.where(ok, cand, thr), jnp.where(ok, n_cand, n_ge)

    thr, n_ge = lax.fori_loop(0, 31, bisect, (thr, n_ge))

    @pl.when(jnp.max(n_ge - k_eff) > 0)
    def _():
        need = k_eff - count(lambda k, _: k > thr)

        def ties_before(limit):
            return count(lambda k, k0: (k == thr) & (k0 + row_iota < limit))

        def grow(it, limit):
            cand = limit + jnp.left_shift(jnp.int32(1), 12 - it)
            return jnp.where(ties_before(cand) <= need, cand, limit)

        limit = lax.fori_loop(0, 13, grow, zero)

        def demote(c, _):
            k0 = chunk_start(c)
            k = keys_ref[pl.ds(k0, K_CHUNK), :]
            keys_ref[pl.ds(k0, K_CHUNK), :] = jnp.where((k == thr) & (k0 + row_iota >= limit), thr - 1, k)
            return 0

        lax.fori_loop(0, n_chunks, demote, 0)

    m_ref[...] = jnp.full(m_ref.shape, _NEG_BIG, _F32)
    acc_ref[...] = jnp.zeros(acc_ref.shape, _F32)

    def attend(c, _):
        k0 = chunk_start(c)
        bias = jnp.where(keys_ref[pl.ds(k0, K_CHUNK), :] >= thr, 0.0, _NEG_BIG)
        ckv = ckv_ref[0, pl.ds(k0, K_CHUNK), :]
        ckvt = ckvt_ref[0, :, pl.ds(k0, K_CHUNK)]
        logit = _dot(ckv, qlt_ref[0, 0]) + jnp.concatenate([bias] * N_HEADS, axis=1)
        m_old = m_ref[...]
        m_new = jnp.maximum(m_old, jnp.max(logit, axis=0, keepdims=True))
        p = jnp.exp2(logit - m_new).astype(_BF16)
        acc_ref[...] = jnp.exp2(m_old - m_new) * acc_ref[...] + _dot(ckvt, p)
        m_ref[...] = m_new
        return 0

    lax.fori_loop(0, n_chunks, attend, 0)

    o_lat_t = (acc_ref[:KV_LATENT, :] / acc_ref[KV_LATENT:KV_LATENT + 1, :]).astype(_BF16)
    out_t = [_dot(wuvt_ref[hd], o_lat_t[:, hd * Q_TILE:(hd + 1) * Q_TILE]) for hd in range(N_HEADS)]
    o_ref[0] = jnp.concatenate(out_t, axis=0).T.astype(_BF16)


def _attn(qlt, qit, wit, ki, ckv, ckvt, wuvt, k_sel):
    batch, _, seq = ckvt.shape
    return pl.pallas_call(
        functools.partial(_attn_kernel, k_sel=k_sel),
        grid=(batch, seq // Q_TILE),
        in_specs=[
            pl.BlockSpec((1, 1, KV_LATENT, N_HEADS * Q_TILE), lambda b, i: (b, i, 0, 0)),
            pl.BlockSpec((1, IDX_HEADS * IDX_DIM, Q_TILE), lambda b, i: (b, 0, i)),
            pl.BlockSpec((1, IDX_HEADS, Q_TILE), lambda b, i: (b, 0, i)),
            pl.BlockSpec((1, seq, IDX_DIM), lambda b, i: (b, 0, 0)),
            pl.BlockSpec((1, seq, KV_LATENT), lambda b, i: (b, 0, 0)),
            pl.BlockSpec((1, KV_ROWS, seq), lambda b, i: (b, 0, 0)),
            _const_spec((N_HEADS, HEAD_DIM, KV_LATENT)),
        ],
        out_specs=pl.BlockSpec((1, Q_TILE, N_HEADS * HEAD_DIM), lambda b, i: (b, i, 0)),
        out_shape=jax.ShapeDtypeStruct((batch, seq, N_HEADS * HEAD_DIM), _BF16),
        scratch_shapes=[pltpu.VMEM((seq, Q_TILE), jnp.int32), pltpu.VMEM((1, N_HEADS * Q_TILE), _F32),
                        pltpu.VMEM((KV_ROWS, N_HEADS * Q_TILE), _F32)],
        compiler_params=pltpu.CompilerParams(dimension_semantics=("arbitrary", "arbitrary"),
                                             vmem_limit_bytes=VMEM_LIMIT_BYTES),
        name="dsa_attn",
    )(qlt, qit, wit, ki, ckv, ckvt, wuvt)


def _merge_kernel(x_ref, attn_ref, pool_ref, halo_ref, sga_ref, sgb_ref, wba_ref, pw_ref, ps_ref, wbp_ref, wo_ref,
                  o_ref, *, tiles_per_seq):
    tile = pl.program_id(0) % tiles_per_seq
    cur = pool_ref[...]
    halo = jnp.where(tile == 0, 0.0, halo_ref[...])
    ext = jnp.concatenate([halo, cur], axis=0)
    t1 = (tile * TOKEN_TILE + 1 + lax.broadcasted_iota(jnp.int32, (TOKEN_TILE, 1), 0)).astype(_F32)
    mixed = []
    win_sum, width = ext, 1
    for g, w in enumerate(POOL_WINDOWS):
        while width < w:
            win_sum = win_sum + pltpu.roll(win_sum, width, 0)
            width *= 2
        cols = slice(g * POOL_GROUP, (g + 1) * POOL_GROUP)
        pooled = win_sum[POOL_HALO:, cols] / jnp.minimum(t1, float(w)) - cur[:, cols]
        mixed.append(_dot(pooled.astype(_BF16), pw_ref[g]))
    mixed = (jnp.concatenate(mixed, axis=-1) * ps_ref[...]).astype(_BF16)
    merged = (sga_ref[...].astype(_F32) * _dot(attn_ref[...], wba_ref[...])
              + sgb_ref[...].astype(_F32) * _dot(mixed, wbp_ref[...]))
    o_ref[...] = x_ref[...] + _dot(merged.astype(_BF16), wo_ref[...])


def _merge(x, attn, pool, sga, sgb, wba, pw, ps, wbp, wo, seq):
    n = x.shape[0]
    tps = seq // TOKEN_TILE
    halo_blocks = TOKEN_TILE // POOL_HALO

    def flat(width):
        return pl.BlockSpec((TOKEN_TILE, width), lambda i: (i, 0))

    return pl.pallas_call(
        functools.partial(_merge_kernel, tiles_per_seq=tps),
        grid=(n // TOKEN_TILE,),
        in_specs=[
            flat(D_MODEL), flat(N_HEADS * HEAD_DIM), flat(POOL_WIDTH),
            pl.BlockSpec((POOL_HALO, POOL_WIDTH), lambda i: (jnp.maximum(i * halo_blocks - 1, 0), 0)),
            flat(D_MODEL), flat(D_MODEL),
            _const_spec((N_HEADS * HEAD_DIM, D_MODEL)), _const_spec((len(POOL_WINDOWS), POOL_GROUP, POOL_GROUP)),
            _const_spec((1, POOL_WIDTH)), _const_spec((POOL_WIDTH, D_MODEL)), _const_spec((D_MODEL, D_MODEL)),
        ],
        out_specs=flat(D_MODEL),
        out_shape=jax.ShapeDtypeStruct((n, D_MODEL), _F32),
        compiler_params=pltpu.CompilerParams(dimension_semantics=("arbitrary",), vmem_limit_bytes=VMEM_LIMIT_BYTES),
        name="merge_out",
    )(x, attn, pool, pool, sga, sgb, wba, pw, ps, wbp, wo)


def _split_w_in(w):
    cuts, off = {}, 0
    for name, width in (("q", N_HEADS * HEAD_DIM), ("ckv", KV_LATENT), ("qi", IDX_HEADS * IDX_DIM), ("ki", IDX_DIM),
                        ("wi", IDX_HEADS), ("pool", POOL_WIDTH), ("ga", D_MODEL), ("gb", D_MODEL)):
        cuts[name] = w[:, off:off + width]
        off += width
    pad_n = jnp.zeros((w.shape[0], LANES - IDX_DIM), w.dtype)
    wn = jnp.concatenate([cuts["ckv"], cuts["pool"], cuts["ga"], cuts["gb"], cuts["ki"], pad_n], axis=1)
    pad_t = jnp.zeros((w.shape[0], BF16_SUBLANES - IDX_HEADS), w.dtype)
    wt = jnp.concatenate([cuts["q"], cuts["qi"], cuts["wi"], pad_t], axis=1).T
    return wn.astype(_BF16), wt.astype(_BF16)


def kernel(x, norm_ffn1, ffn1_gate, ffn1_up, ffn1_down, norm_mix, w_in, norm_kv, w_uk, w_uv, pool_w, pool_scale,
           w_branch_attn, w_branch_pool, w_out, norm_ffn2, ffn2_gate, ffn2_up, ffn2_down, norm_final):
    batch, seq, _ = x.shape
    depth = norm_ffn1.shape[0]
    assert seq % TOKEN_TILE == 0 and seq % Q_TILE == 0
    k_sel = min(TOPK_MAX, seq // 4)
    n = batch * seq
    bf = lambda a: a.astype(_BF16)
    row = lambda a: a.reshape(1, -1)
    gf = row(norm_final)
    h = x.reshape(n, D_MODEL)
    for i in range(depth):
        h = _ffn(h, row(norm_ffn1[i]), bf(ffn1_gate[i]), bf(ffn1_up[i]), bf(ffn1_down[i]), gf, False)
        wn, wt = _split_w_in(w_in[i])
        wuk_h = bf(jnp.transpose(w_uk[i], (1, 0, 2)))
        wuvt_h = bf(jnp.transpose(w_uv[i], (1, 2, 0)))
        qlt, ckv, ckvt, qit, ki, wit, pool, sga, sgb = _proj(h, row(norm_mix[i]), wn, wt, row(norm_kv[i]), wuk_h,
                                                             batch, seq)
        attn = _attn(qlt, qit, wit, ki.reshape(batch, seq, -1), ckv.reshape(batch, seq, -1), ckvt, wuvt_h, k_sel)
        h = _merge(h, attn.reshape(n, -1), pool, sga, sgb, bf(w_branch_attn[i]), bf(pool_w[i]), row(pool_scale[i]),
                   bf(w_branch_pool[i]), bf(w_out[i]), seq)
        h = _ffn(h, row(norm_ffn2[i]), bf(ffn2_gate[i]), bf(ffn2_up[i]), bf(ffn2_down[i]), gf, i == depth - 1)
    return h.reshape(batch, seq, D_MODEL)
```

```python
import functools

import jax
import jax.numpy as jnp
from jax import lax
from jax.experimental import pallas as pl
from jax.experimental.pallas import tpu as pltpu

D_MODEL = 1024
N_HEADS = 8
HEAD_DIM = 64
KV_LATENT = 128
ATTN_SCALE = HEAD_DIM ** -0.5
LOG2_E = 1.4426950408889634
IDX_HEADS = 8
IDX_DIM = 64
IDX_SCALE = (IDX_HEADS ** -0.5) * (IDX_DIM ** -0.5)
TOPK_MAX = 256
POOL_WINDOWS = (2, 4, 8, 16)
POOL_GROUP = 128
POOL_WIDTH = POOL_GROUP * len(POOL_WINDOWS)
POOL_HALO = 16
D_FF = 2816
EPS = 1e-6

LANES = 128
BF16_SUBLANES = 16
KV_ROWS = KV_LATENT + BF16_SUBLANES
VMEM_LIMIT_BYTES = 56 * 1024 * 1024

TOKEN_TILE = 512
FF_CHUNK = 256
Q_TILE = 256
K_CHUNK = Q_TILE
COUNT_ROWS = 32

_N_CKV = 0
_N_POOL = _N_CKV + KV_LATENT
_N_GA = _N_POOL + POOL_WIDTH
_N_GB = _N_GA + D_MODEL
_N_KI = _N_GB + D_MODEL
_N_TOTAL = _N_KI + LANES
_T_Q = 0
_T_QI = _T_Q + N_HEADS * HEAD_DIM
_T_WI = _T_QI + IDX_HEADS * IDX_DIM
_T_TOTAL = _T_WI + BF16_SUBLANES

_F32 = jnp.float32
_BF16 = jnp.bfloat16
_INT_MIN = -(2 ** 31)
_HALF = 2 ** 15
_NEG_BIG = float(jnp.finfo(jnp.float32).min)


def _const_spec(shape):
    return pl.BlockSpec(shape, lambda *_: (0,) * len(shape), pipeline_mode=pl.Buffered(1))


def _rms(x, g):
    return x * lax.rsqrt(jnp.mean(x * x, axis=-1, keepdims=True) + EPS) * g


def _dot(a, b):
    return jnp.dot(a, b, preferred_element_type=_F32)


def _ffn_kernel(x_ref, g_ref, wg_ref, wu_ref, wd_ref, gf_ref, o_ref, *, final_norm):
    x = x_ref[...]
    h = _rms(x, g_ref[...]).astype(_BF16)
    acc = jnp.zeros(x.shape, _F32)
    for c in range(D_FF // FF_CHUNK):
        sl = slice(c * FF_CHUNK, (c + 1) * FF_CHUNK)
        gate = _dot(h, wg_ref[:, sl])
        up = _dot(h, wu_ref[:, sl])
        act = (gate * jax.nn.sigmoid(gate) * up).astype(_BF16)
        acc = acc + _dot(act, wd_ref[sl, :])
    y = x + 0.5 * acc
    if final_norm:
        y = _rms(y, gf_ref[...])
    o_ref[...] = y


def _ffn(x, g, wg, wu, wd, gf, final_norm):
    n = x.shape[0]
    tile = pl.BlockSpec((TOKEN_TILE, D_MODEL), lambda i: (i, 0))
    return pl.pallas_call(
        functools.partial(_ffn_kernel, final_norm=final_norm),
        grid=(n // TOKEN_TILE,),
        in_specs=[tile, _const_spec((1, D_MODEL)), _const_spec((D_MODEL, D_FF)), _const_spec((D_MODEL, D_FF)),
                  _const_spec((D_FF, D_MODEL)), _const_spec((1, D_MODEL))],
        out_specs=tile,
        out_shape=jax.ShapeDtypeStruct((n, D_MODEL), _F32),
        compiler_params=pltpu.CompilerParams(dimension_semantics=("arbitrary",), vmem_limit_bytes=VMEM_LIMIT_BYTES),
        name="ffn",
    )(x, g, wg, wu, wd, gf)


def _proj_kernel(x_ref, g_ref, wn_ref, wt_ref, gkv_ref, wuk_ref, qlt_ref, ckv_ref, ckvt_ref, qit_ref, ki_ref, wit_ref,
                 pool_ref, sga_ref, sgb_ref):
    h32 = _rms(x_ref[...], g_ref[...])
    h = h32.astype(_BF16)
    ht = h32.T.astype(_BF16)

    def z(lo, hi):
        return _dot(h, wn_ref[:, lo:hi])

    def zt(lo, hi):
        return _dot(wt_ref[lo:hi, :], ht)

    ckv = _rms(z(_N_CKV, _N_POOL), gkv_ref[...])
    ckv_ref[...] = ckv.astype(_BF16)
    ones_tile = (lax.broadcasted_iota(jnp.int32, (BF16_SUBLANES, TOKEN_TILE), 0) == 0).astype(_F32)
    ckvt_ref[0] = jnp.concatenate([ckv.T, ones_tile], axis=0).astype(_BF16)
    pool_ref[...] = z(_N_POOL, _N_GA)
    sga_ref[...] = jax.nn.sigmoid(z(_N_GA, _N_GB)).astype(_BF16)
    sgb_ref[...] = jax.nn.sigmoid(z(_N_GB, _N_KI)).astype(_BF16)
    ki_ref[...] = z(_N_KI, _N_TOTAL)[:, :IDX_DIM].astype(_BF16)

    qt = zt(_T_Q, _T_QI).astype(_BF16)
    for hd in range(N_HEADS):
        qlt = (_dot(wuk_ref[hd], qt[hd * HEAD_DIM:(hd + 1) * HEAD_DIM, :]) * (ATTN_SCALE * LOG2_E)).astype(_BF16)
        for j in range(TOKEN_TILE // Q_TILE):
            qlt_ref[0, j, :, hd * Q_TILE:(hd + 1) * Q_TILE] = qlt[:, j * Q_TILE:(j + 1) * Q_TILE]
    qit_ref[0] = zt(_T_QI, _T_WI).astype(_BF16)
    wit_ref[0] = zt(_T_WI, _T_TOTAL)[:IDX_HEADS, :] * IDX_SCALE


def _proj(x, g, wn, wt, gkv, wuk, batch, seq):
    n = x.shape[0]
    tps = seq // TOKEN_TILE

    def flat(width):
        return pl.BlockSpec((TOKEN_TILE, width), lambda i: (i, 0))

    def feat(rows):
        return pl.BlockSpec((1, rows, TOKEN_TILE), lambda i: (i // tps, 0, i % tps))

    out_shape = [
        jax.ShapeDtypeStruct((batch, seq // Q_TILE, KV_LATENT, N_HEADS * Q_TILE), _BF16),
        jax.ShapeDtypeStruct((n, KV_LATENT), _BF16),
        jax.ShapeDtypeStruct((batch, KV_ROWS, seq), _BF16),
        jax.ShapeDtypeStruct((batch, IDX_HEADS * IDX_DIM, seq), _BF16),
        jax.ShapeDtypeStruct((n, IDX_DIM), _BF16),
        jax.ShapeDtypeStruct((batch, IDX_HEADS, seq), _F32),
        jax.ShapeDtypeStruct((n, POOL_WIDTH), _F32),
        jax.ShapeDtypeStruct((n, D_MODEL), _BF16),
        jax.ShapeDtypeStruct((n, D_MODEL), _BF16),
    ]
    out_specs = [
        pl.BlockSpec((1, TOKEN_TILE // Q_TILE, KV_LATENT, N_HEADS * Q_TILE), lambda i: (i // tps, i % tps, 0, 0)),
        flat(KV_LATENT), feat(KV_ROWS), feat(IDX_HEADS * IDX_DIM), flat(IDX_DIM), feat(IDX_HEADS),
        flat(POOL_WIDTH), flat(D_MODEL), flat(D_MODEL),
    ]
    return pl.pallas_call(
        _proj_kernel,
        grid=(n // TOKEN_TILE,),
        in_specs=[flat(D_MODEL), _const_spec((1, D_MODEL)), _const_spec((D_MODEL, _N_TOTAL)),
                  _const_spec((_T_TOTAL, D_MODEL)), _const_spec((1, KV_LATENT)),
                  _const_spec((N_HEADS, KV_LATENT, HEAD_DIM))],
        out_specs=out_specs,
        out_shape=out_shape,
        compiler_params=pltpu.CompilerParams(dimension_semantics=("arbitrary",), vmem_limit_bytes=VMEM_LIMIT_BYTES),
        name="mix_proj",
    )(x, g, wn, wt, gkv, wuk)


def _attn_kernel(qlt_ref, qit_ref, wit_ref, ki_ref, ckv_ref, ckvt_ref, wuvt_ref, o_ref, keys_ref, hi_ref, lo_ref, m_ref,
                 acc_ref, *, k_sel):
    qb = pl.program_id(1)
    n_chunks = qb + 1
    q_pos = qb * Q_TILE + lax.broadcasted_iota(jnp.int32, (1, Q_TILE), 1)
    k_eff = jnp.minimum(k_sel, q_pos + 1)
    row_iota = lax.broadcasted_iota(jnp.int32, (K_CHUNK, Q_TILE), 0)

    def chunk_start(c):
        return pl.multiple_of(c * K_CHUNK, K_CHUNK)

    def score_chunk(c, _):
        k0 = chunk_start(c)
        kc = ki_ref[0, pl.ds(k0, K_CHUNK), :]
        s = jnp.zeros((K_CHUNK, Q_TILE), _F32)
        for j in range(IDX_HEADS):
            r = _dot(kc, qit_ref[0, j * IDX_DIM:(j + 1) * IDX_DIM, :])
            s = s + jnp.maximum(r, 0.0) * wit_ref[0, j:j + 1, :]
        bits = pltpu.bitcast(s, jnp.int32)
        key = bits ^ ((bits >> 31) & 0x7FFFFFFF)
        key = jnp.where(k0 + row_iota <= q_pos, key, _INT_MIN)
        keys_ref[pl.ds(k0, K_CHUNK), :] = key
        hi_ref[pl.ds(k0, K_CHUNK), :] = (key >> 16).astype(jnp.int16)
        lo_ref[pl.ds(k0, K_CHUNK), :] = ((key & 0xFFFF) - _HALF).astype(jnp.int16)
        return 0

    lax.fori_loop(0, n_chunks, score_chunk, 0)

    def count16(ref, pred):
        def body(c, acc):
            hit = pred(ref[pl.ds(chunk_start(c), K_CHUNK), :]).astype(jnp.int16)
            for r in range(K_CHUNK // COUNT_ROWS):
                acc = acc + hit[r * COUNT_ROWS:(r + 1) * COUNT_ROWS, :]
            return acc

        acc = lax.fori_loop(0, n_chunks, body, jnp.zeros((COUNT_ROWS, Q_TILE), jnp.int16))
        return jnp.sum(acc.astype(jnp.int32), axis=0, keepdims=True)

    def kth_largest16(ref, k):
        n_nonneg = count16(ref, lambda v: v >= jnp.zeros((1, Q_TILE), jnp.int16))
        sign_ok = n_nonneg >= k
        best = jnp.where(sign_ok, 0, -_HALF)
        n_best = jnp.where(sign_ok, n_nonneg, n_chunks * K_CHUNK)

        def bisect(it, carry):
            best, n_best = carry
            cand = best + jnp.left_shift(jnp.int32(1), 14 - it)
            cand16 = cand.astype(jnp.int16)
            n_cand = count16(ref, lambda v: v >= cand16)
            ok = n_cand >= k
            return jnp.where(ok, cand, best), jnp.where(ok, n_cand, n_best)

        return lax.fori_loop(0, 15, bisect, (best, n_best))

    def count(pred):
        def body(c, acc):
            k0 = chunk_start(c)
            hit = pred(keys_ref[pl.ds(k0, K_CHUNK), :], k0).astype(jnp.int32)
            return acc + jnp.sum(hit.reshape(K_CHUNK // COUNT_ROWS, COUNT_ROWS, Q_TILE), axis=0)

        acc = lax.fori_loop(0, n_chunks, body, jnp.zeros((COUNT_ROWS, Q_TILE), jnp.int32))
        return jnp.sum(acc, axis=0, keepdims=True)

    hi_thr, n_ge_hi = kth_largest16(hi_ref, k_eff)
    hi_thr16 = hi_thr.astype(jnp.int16)
    n_gt_hi = count16(hi_ref, lambda v: v > hi_thr16)

    def restrict(c, _):
        rows = pl.ds(chunk_start(c), K_CHUNK)
        lo_ref[rows, :] = jnp.where(hi_ref[rows, :] == hi_thr16, lo_ref[rows, :], jnp.int16(-_HALF))
        return 0

    lax.fori_loop(0, n_chunks, restrict, 0)
    lo_thr, n_ge_lo = kth_largest16(lo_ref, k_eff - n_gt_hi)
    thr = hi_thr * (2 * _HALF) + (lo_thr + _HALF)
    n_ge = jnp.where(lo_thr > -_HALF, n_gt_hi + n_ge_lo, n_ge_hi)
    zero = jnp.zeros((1, Q_TILE), jnp.int32)

    @pl.when(jnp.max(n_ge - k_eff) > 0)
    def _():
        need = k_eff - count(lambda k, _: k > thr)

        def ties_before(limit):
            return count(lambda k, k0: (k == thr) & (k0 + row_iota < limit))

        def grow(it, limit):
            cand = limit + jnp.left_shift(jnp.int32(1), 12 - it)
            return jnp.where(ties_before(cand) <= need, cand, limit)

        limit = lax.fori_loop(0, 13, grow, zero)

        def demote(c, _):
            k0 = chunk_start(c)
            k = keys_ref[pl.ds(k0, K_CHUNK), :]
            keys_ref[pl.ds(k0, K_CHUNK), :] = jnp.where((k == thr) & (k0 + row_iota >= limit), thr - 1, k)
            return 0

        lax.fori_loop(0, n_chunks, demote, 0)

    m_ref[...] = jnp.full(m_ref.shape, _NEG_BIG, _F32)
    acc_ref[...] = jnp.zeros(acc_ref.shape, _F32)

    def attend(c, _):
        k0 = chunk_start(c)
        bias = jnp.where(keys_ref[pl.ds(k0, K_CHUNK), :] >= thr, 0.0, _NEG_BIG)
        ckv = ckv_ref[0, pl.ds(k0, K_CHUNK), :]
        ckvt = ckvt_ref[0, :, pl.ds(k0, K_CHUNK)]
        logit = _dot(ckv, qlt_ref[0, 0]) + jnp.concatenate([bias] * N_HEADS, axis=1)
        m_old = m_ref[...]
        m_new = jnp.maximum(m_old, jnp.max(logit, axis=0, keepdims=True))
        p = jnp.exp2(logit - m_new).astype(_BF16)
        acc_ref[...] = jnp.exp2(m_old - m_new) * acc_ref[...] + _dot(ckvt, p)
        m_ref[...] = m_new
        return 0

    lax.fori_loop(0, n_chunks, attend, 0)

    o_lat_t = (acc_ref[:KV_LATENT, :] / acc_ref[KV_LATENT:KV_LATENT + 1, :]).astype(_BF16)
    out_t = [_dot(wuvt_ref[hd], o_lat_t[:, hd * Q_TILE:(hd + 1) * Q_TILE]) for hd in range(N_HEADS)]
    o_ref[0] = jnp.concatenate(out_t, axis=0).T.astype(_BF16)


def _attn(qlt, qit, wit, ki, ckv, ckvt, wuvt, k_sel):
    batch, _, seq = ckvt.shape
    return pl.pallas_call(
        functools.partial(_attn_kernel, k_sel=k_sel),
        grid=(batch, seq // Q_TILE),
        in_specs=[
            pl.BlockSpec((1, 1, KV_LATENT, N_HEADS * Q_TILE), lambda b, i: (b, i, 0, 0)),
            pl.BlockSpec((1, IDX_HEADS * IDX_DIM, Q_TILE), lambda b, i: (b, 0, i)),
            pl.BlockSpec((1, IDX_HEADS, Q_TILE), lambda b, i: (b, 0, i)),
            pl.BlockSpec((1, seq, IDX_DIM), lambda b, i: (b, 0, 0)),
            pl.BlockSpec((1, seq, KV_LATENT), lambda b, i: (b, 0, 0)),
            pl.BlockSpec((1, KV_ROWS, seq), lambda b, i: (b, 0, 0)),
            _const_spec((N_HEADS, HEAD_DIM, KV_LATENT)),
        ],
        out_specs=pl.BlockSpec((1, Q_TILE, N_HEADS * HEAD_DIM), lambda b, i: (b, i, 0)),
        out_shape=jax.ShapeDtypeStruct((batch, seq, N_HEADS * HEAD_DIM), _BF16),
        scratch_shapes=[pltpu.VMEM((seq, Q_TILE), jnp.int32), pltpu.VMEM((seq, Q_TILE), jnp.int16),
                        pltpu.VMEM((seq, Q_TILE), jnp.int16), pltpu.VMEM((1, N_HEADS * Q_TILE), _F32),
                        pltpu.VMEM((KV_ROWS, N_HEADS * Q_TILE), _F32)],
        compiler_params=pltpu.CompilerParams(dimension_semantics=("arbitrary", "arbitrary"),
                                             vmem_limit_bytes=VMEM_LIMIT_BYTES),
        name="dsa_attn",
    )(qlt, qit, wit, ki, ckv, ckvt, wuvt)


def _merge_kernel(x_ref, attn_ref, pool_ref, halo_ref, sga_ref, sgb_ref, wba_ref, pw_ref, ps_ref, wbp_ref, wo_ref,
                  o_ref, *, tiles_per_seq):
    tile = pl.program_id(0) % tiles_per_seq
    cur = pool_ref[...]
    halo = jnp.where(tile == 0, 0.0, halo_ref[...])
    ext = jnp.concatenate([halo, cur], axis=0)
    t1 = (tile * TOKEN_TILE + 1 + lax.broadcasted_iota(jnp.int32, (TOKEN_TILE, 1), 0)).astype(_F32)
    mixed = []
    win_sum, width = ext, 1
    for g, w in enumerate(POOL_WINDOWS):
        while width < w:
            win_sum = win_sum + pltpu.roll(win_sum, width, 0)
            width *= 2
        cols = slice(g * POOL_GROUP, (g + 1) * POOL_GROUP)
        pooled = win_sum[POOL_HALO:, cols] / jnp.minimum(t1, float(w)) - cur[:, cols]
        mixed.append(_dot(pooled.astype(_BF16), pw_ref[g]))
    mixed = (jnp.concatenate(mixed, axis=-1) * ps_ref[...]).astype(_BF16)
    merged = (sga_ref[...].astype(_F32) * _dot(attn_ref[...], wba_ref[...])
              + sgb_ref[...].astype(_F32) * _dot(mixed, wbp_ref[...]))
    o_ref[...] = x_ref[...] + _dot(merged.astype(_BF16), wo_ref[...])


def _merge(x, attn, pool, sga, sgb, wba, pw, ps, wbp, wo, seq):
    n = x.shape[0]
    tps = seq // TOKEN_TILE
    halo_blocks = TOKEN_TILE // POOL_HALO

    def flat(width):
        return pl.BlockSpec((TOKEN_TILE, width), lambda i: (i, 0))

    return pl.pallas_call(
        functools.partial(_merge_kernel, tiles_per_seq=tps),
        grid=(n // TOKEN_TILE,),
        in_specs=[
            flat(D_MODEL), flat(N_HEADS * HEAD_DIM), flat(POOL_WIDTH),
            pl.BlockSpec((POOL_HALO, POOL_WIDTH), lambda i: (jnp.maximum(i * halo_blocks - 1, 0), 0)),
            flat(D_MODEL), flat(D_MODEL),
            _const_spec((N_HEADS * HEAD_DIM, D_MODEL)), _const_spec((len(POOL_WINDOWS), POOL_GROUP, POOL_GROUP)),
            _const_spec((1, POOL_WIDTH)), _const_spec((POOL_WIDTH, D_MODEL)), _const_spec((D_MODEL, D_MODEL)),
        ],
        out_specs=flat(D_MODEL),
        out_shape=jax.ShapeDtypeStruct((n, D_MODEL), _F32),
        compiler_params=pltpu.CompilerParams(dimension_semantics=("arbitrary",), vmem_limit_bytes=VMEM_LIMIT_BYTES),
        name="merge_out",
    )(x, attn, pool, pool, sga, sgb, wba, pw, ps, wbp, wo)


def _split_w_in(w):
    cuts, off = {}, 0
    for name, width in (("q", N_HEADS * HEAD_DIM), ("ckv", KV_LATENT), ("qi", IDX_HEADS * IDX_DIM), ("ki", IDX_DIM),
                        ("wi", IDX_HEADS), ("pool", POOL_WIDTH), ("ga", D_MODEL), ("gb", D_MODEL)):
        cuts[name] = w[:, off:off + width]
        off += width
    pad_n = jnp.zeros((w.shape[0], LANES - IDX_DIM), w.dtype)
    wn = jnp.concatenate([cuts["ckv"], cuts["pool"], cuts["ga"], cuts["gb"], cuts["ki"], pad_n], axis=1)
    pad_t = jnp.zeros((w.shape[0], BF16_SUBLANES - IDX_HEADS), w.dtype)
    wt = jnp.concatenate([cuts["q"], cuts["qi"], cuts["wi"], pad_t], axis=1).T
    return wn.astype(_BF16), wt.astype(_BF16)


def kernel(x, norm_ffn1, ffn1_gate, ffn1_up, ffn1_down, norm_mix, w_in, norm_kv, w_uk, w_uv, pool_w, pool_scale,
           w_branch_attn, w_branch_pool, w_out, norm_ffn2, ffn2_gate, ffn2_up, ffn2_down, norm_final):
    batch, seq, _ = x.shape
    depth = norm_ffn1.shape[0]
    assert seq % TOKEN_TILE == 0 and seq % Q_TILE == 0
    k_sel = min(TOPK_MAX, seq // 4)
    n = batch * seq
    bf = lambda a: a.astype(_BF16)
    row = lambda a: a.reshape(1, -1)
    gf = row(norm_final)
    h = x.reshape(n, D_MODEL)
    for i in range(depth):
        h = _ffn(h, row(norm_ffn1[i]), bf(ffn1_gate[i]), bf(ffn1_up[i]), bf(ffn1_down[i]), gf, False)
        wn, wt = _split_w_in(w_in[i])
        wuk_h = bf(jnp.transpose(w_uk[i], (1, 0, 2)))
        wuvt_h = bf(jnp.transpose(w_uv[i], (1, 2, 0)))
        qlt, ckv, ckvt, qit, ki, wit, pool, sga, sgb = _proj(h, row(norm_mix[i]), wn, wt, row(norm_kv[i]), wuk_h,
                                                             batch, seq)
        attn = _attn(qlt, qit, wit, ki.reshape(batch, seq, -1), ckv.reshape(batch, seq, -1), ckvt, wuvt_h, k_sel)
        h = _merge(h, attn.reshape(n, -1), pool, sga, sgb, bf(w_branch_attn[i]), bf(pool_w[i]), row(pool_scale[i]),
                   bf(w_branch_pool[i]), bf(w_out[i]), seq)
        h = _ffn(h, row(norm_ffn2[i]), bf(ffn2_gate[i]), bf(ffn2_up[i]), bf(ffn2_down[i]), gf, i == depth - 1)
    return h.reshape(batch, seq, D_MODEL)
```

```python
import functools

import jax
import jax.numpy as jnp
from jax import lax
from jax.experimental import pallas as pl
from jax.experimental.pallas import tpu as pltpu

D_MODEL = 1024
N_HEADS = 8
HEAD_DIM = 64
KV_LATENT = 128
ATTN_SCALE = HEAD_DIM ** -0.5
LOG2_E = 1.4426950408889634
IDX_HEADS = 8
IDX_DIM = 64
IDX_SCALE = (IDX_HEADS ** -0.5) * (IDX_DIM ** -0.5)
TOPK_MAX = 256
POOL_WINDOWS = (2, 4, 8, 16)
POOL_GROUP = 128
POOL_WIDTH = POOL_GROUP * len(POOL_WINDOWS)
POOL_HALO = 16
D_FF = 2816
EPS = 1e-6

LANES = 128
BF16_SUBLANES = 16
KV_ROWS = KV_LATENT + BF16_SUBLANES
VMEM_LIMIT_BYTES = 56 * 1024 * 1024

TOKEN_TILE = 512
FF_CHUNK = 256
Q_TILE = 256
K_CHUNK = Q_TILE
COUNT_ROWS = 32
SEARCH_STEP = 2

_N_CKV = 0
_N_POOL = _N_CKV + KV_LATENT
_N_GA = _N_POOL + POOL_WIDTH
_N_GB = _N_GA + D_MODEL
_N_KI = _N_GB + D_MODEL
_N_TOTAL = _N_KI + LANES
_T_Q = 0
_T_QI = _T_Q + N_HEADS * HEAD_DIM
_T_WI = _T_QI + IDX_HEADS * IDX_DIM
_T_TOTAL = _T_WI + BF16_SUBLANES

_F32 = jnp.float32
_BF16 = jnp.bfloat16
_INT_MIN = -(2 ** 31)
_HALF = 2 ** 15
_NEG_BIG = float(jnp.finfo(jnp.float32).min)


def _const_spec(shape):
    return pl.BlockSpec(shape, lambda *_: (0,) * len(shape), pipeline_mode=pl.Buffered(1))


def _rms(x, g):
    return x * lax.rsqrt(jnp.mean(x * x, axis=-1, keepdims=True) + EPS) * g


def _dot(a, b):
    return jnp.dot(a, b, preferred_element_type=_F32)


def _ffn_kernel(x_ref, g_ref, wg_ref, wu_ref, wd_ref, gf_ref, o_ref, *, final_norm):
    x = x_ref[...]
    h = _rms(x, g_ref[...]).astype(_BF16)
    acc = jnp.zeros(x.shape, _F32)
    for c in range(D_FF // FF_CHUNK):
        sl = slice(c * FF_CHUNK, (c + 1) * FF_CHUNK)
        gate = _dot(h, wg_ref[:, sl])
        up = _dot(h, wu_ref[:, sl])
        act = (gate * jax.nn.sigmoid(gate) * up).astype(_BF16)
        acc = acc + _dot(act, wd_ref[sl, :])
    y = x + 0.5 * acc
    if final_norm:
        y = _rms(y, gf_ref[...])
    o_ref[...] = y


def _ffn(x, g, wg, wu, wd, gf, final_norm):
    n = x.shape[0]
    tile = pl.BlockSpec((TOKEN_TILE, D_MODEL), lambda i: (i, 0))
    return pl.pallas_call(
        functools.partial(_ffn_kernel, final_norm=final_norm),
        grid=(n // TOKEN_TILE,),
        in_specs=[tile, _const_spec((1, D_MODEL)), _const_spec((D_MODEL, D_FF)), _const_spec((D_MODEL, D_FF)),
                  _const_spec((D_FF, D_MODEL)), _const_spec((1, D_MODEL))],
        out_specs=tile,
        out_shape=jax.ShapeDtypeStruct((n, D_MODEL), _F32),
        compiler_params=pltpu.CompilerParams(dimension_semantics=("arbitrary",), vmem_limit_bytes=VMEM_LIMIT_BYTES),
        name="ffn",
    )(x, g, wg, wu, wd, gf)


def _proj_kernel(x_ref, g_ref, wn_ref, wt_ref, gkv_ref, wuk_ref, qlt_ref, ckv_ref, ckvt_ref, qit_ref, ki_ref, wit_ref,
                 pool_ref, sga_ref, sgb_ref):
    h32 = _rms(x_ref[...], g_ref[...])
    h = h32.astype(_BF16)
    ht = h32.T.astype(_BF16)

    def z(lo, hi):
        return _dot(h, wn_ref[:, lo:hi])

    def zt(lo, hi):
        return _dot(wt_ref[lo:hi, :], ht)

    ckv = _rms(z(_N_CKV, _N_POOL), gkv_ref[...])
    ckv_ref[...] = ckv.astype(_BF16)
    ones_tile = (lax.broadcasted_iota(jnp.int32, (BF16_SUBLANES, TOKEN_TILE), 0) == 0).astype(_F32)
    ckvt_ref[0] = jnp.concatenate([ckv.T, ones_tile], axis=0).astype(_BF16)
    pool_ref[...] = z(_N_POOL, _N_GA)
    sga_ref[...] = jax.nn.sigmoid(z(_N_GA, _N_GB)).astype(_BF16)
    sgb_ref[...] = jax.nn.sigmoid(z(_N_GB, _N_KI)).astype(_BF16)
    ki_ref[...] = z(_N_KI, _N_TOTAL)[:, :IDX_DIM].astype(_BF16)

    qt = zt(_T_Q, _T_QI).astype(_BF16)
    for hd in range(N_HEADS):
        qlt = (_dot(wuk_ref[hd], qt[hd * HEAD_DIM:(hd + 1) * HEAD_DIM, :]) * (ATTN_SCALE * LOG2_E)).astype(_BF16)
        for j in range(TOKEN_TILE // Q_TILE):
            qlt_ref[0, j, :, hd * Q_TILE:(hd + 1) * Q_TILE] = qlt[:, j * Q_TILE:(j + 1) * Q_TILE]
    qit_ref[0] = zt(_T_QI, _T_WI).astype(_BF16)
    wit_ref[0] = zt(_T_WI, _T_TOTAL)[:IDX_HEADS, :] * IDX_SCALE


def _proj(x, g, wn, wt, gkv, wuk, batch, seq):
    n = x.shape[0]
    tps = seq // TOKEN_TILE

    def flat(width):
        return pl.BlockSpec((TOKEN_TILE, width), lambda i: (i, 0))

    def feat(rows):
        return pl.BlockSpec((1, rows, TOKEN_TILE), lambda i: (i // tps, 0, i % tps))

    out_shape = [
        jax.ShapeDtypeStruct((batch, seq // Q_TILE, KV_LATENT, N_HEADS * Q_TILE), _BF16),
        jax.ShapeDtypeStruct((n, KV_LATENT), _BF16),
        jax.ShapeDtypeStruct((batch, KV_ROWS, seq), _BF16),
        jax.ShapeDtypeStruct((batch, IDX_HEADS * IDX_DIM, seq), _BF16),
        jax.ShapeDtypeStruct((n, IDX_DIM), _BF16),
        jax.ShapeDtypeStruct((batch, IDX_HEADS, seq), _F32),
        jax.ShapeDtypeStruct((n, POOL_WIDTH), _F32),
        jax.ShapeDtypeStruct((n, D_MODEL), _BF16),
        jax.ShapeDtypeStruct((n, D_MODEL), _BF16),
    ]
    out_specs = [
        pl.BlockSpec((1, TOKEN_TILE // Q_TILE, KV_LATENT, N_HEADS * Q_TILE), lambda i: (i // tps, i % tps, 0, 0)),
        flat(KV_LATENT), feat(KV_ROWS), feat(IDX_HEADS * IDX_DIM), flat(IDX_DIM), feat(IDX_HEADS),
        flat(POOL_WIDTH), flat(D_MODEL), flat(D_MODEL),
    ]
    return pl.pallas_call(
        _proj_kernel,
        grid=(n // TOKEN_TILE,),
        in_specs=[flat(D_MODEL), _const_spec((1, D_MODEL)), _const_spec((D_MODEL, _N_TOTAL)),
                  _const_spec((_T_TOTAL, D_MODEL)), _const_spec((1, KV_LATENT)),
                  _const_spec((N_HEADS, KV_LATENT, HEAD_DIM))],
        out_specs=out_specs,
        out_shape=out_shape,
        compiler_params=pltpu.CompilerParams(dimension_semantics=("arbitrary",), vmem_limit_bytes=VMEM_LIMIT_BYTES),
        name="mix_proj",
    )(x, g, wn, wt, gkv, wuk)


def _attn_kernel(qlt_ref, qit_ref, wit_ref, ki_ref, ckv_ref, ckvt_ref, wuvt_ref, o_ref, keys_ref, hi_ref, lo_ref,
                 thr_ref, nge_ref, m_ref, acc_ref, *, k_sel):
    qb = pl.program_id(1)
    n_chunks = qb + 1
    q_pos = qb * Q_TILE + lax.broadcasted_iota(jnp.int32, (1, Q_TILE), 1)
    k_eff = jnp.minimum(k_sel, q_pos + 1)
    row_iota = lax.broadcasted_iota(jnp.int32, (K_CHUNK, Q_TILE), 0)

    def chunk_start(c):
        return pl.multiple_of(c * K_CHUNK, K_CHUNK)

    def score_chunk(c, _):
        k0 = chunk_start(c)
        kc = ki_ref[0, pl.ds(k0, K_CHUNK), :]
        s = jnp.zeros((K_CHUNK, Q_TILE), _F32)
        for j in range(IDX_HEADS):
            r = _dot(kc, qit_ref[0, j * IDX_DIM:(j + 1) * IDX_DIM, :])
            s = s + jnp.maximum(r, 0.0) * wit_ref[0, j:j + 1, :]
        bits = pltpu.bitcast(s, jnp.int32)
        key = bits ^ ((bits >> 31) & 0x7FFFFFFF)
        key = jnp.where(k0 + row_iota <= q_pos, key, _INT_MIN)
        keys_ref[pl.ds(k0, K_CHUNK), :] = key
        hi_ref[pl.ds(k0, K_CHUNK), :] = (key >> 16).astype(jnp.int16)
        lo_ref[pl.ds(k0, K_CHUNK), :] = ((key & 0xFFFF) - _HALF).astype(jnp.int16)
        return 0

    lax.fori_loop(0, n_chunks, score_chunk, 0)

    def search(nc):
        def count16(ref, pred):
            acc = jnp.zeros((COUNT_ROWS, Q_TILE), jnp.int16)
            for c in range(nc):
                hit = pred(ref[c * K_CHUNK:(c + 1) * K_CHUNK, :]).astype(jnp.int16)
                parts = [hit[r * COUNT_ROWS:(r + 1) * COUNT_ROWS, :] for r in range(K_CHUNK // COUNT_ROWS)]
                while len(parts) > 1:
                    parts = [x + y for x, y in zip(parts[::2], parts[1::2])]
                acc = acc + parts[0]
            return jnp.sum(acc.astype(jnp.int32), axis=0, keepdims=True)

        def kth_largest16(ref, k):
            n_nonneg = count16(ref, lambda v: v >= jnp.zeros((1, Q_TILE), jnp.int16))
            sign_ok = n_nonneg >= k
            best = jnp.where(sign_ok, 0, -_HALF)
            n_best = jnp.where(sign_ok, n_nonneg, nc * K_CHUNK)

            def bisect(it, carry):
                best, n_best = carry
                cand = best + jnp.left_shift(jnp.int32(1), 14 - it)
                cand16 = cand.astype(jnp.int16)
                n_cand = count16(ref, lambda v: v >= cand16)
                ok = n_cand >= k
                return jnp.where(ok, cand, best), jnp.where(ok, n_cand, n_best)

            return lax.fori_loop(0, 15, bisect, (best, n_best))

        hi_thr, n_ge_hi = kth_largest16(hi_ref, k_eff)
        hi_thr16 = hi_thr.astype(jnp.int16)
        n_gt_hi = count16(hi_ref, lambda v: v > hi_thr16)
        for c in range(nc):
            rows = slice(c * K_CHUNK, (c + 1) * K_CHUNK)
            lo_ref[rows, :] = jnp.where(hi_ref[rows, :] == hi_thr16, lo_ref[rows, :], jnp.int16(-_HALF))
        lo_thr, n_ge_lo = kth_largest16(lo_ref, k_eff - n_gt_hi)
        thr_ref[...] = hi_thr * (2 * _HALF) + (lo_thr + _HALF)
        nge_ref[...] = jnp.where(lo_thr > -_HALF, n_gt_hi + n_ge_lo, n_ge_hi)

    n_max = keys_ref.shape[0] // K_CHUNK
    lowest = jnp.full((K_CHUNK, Q_TILE), -_HALF, jnp.int16)
    for nc in sorted(set(range(SEARCH_STEP, n_max + 1, SEARCH_STEP)) | {n_max}):
        @pl.when((n_chunks > nc - SEARCH_STEP) & (n_chunks <= nc))
        def _(nc=nc):
            for c in range(max(nc - SEARCH_STEP + 1, 1), nc):
                @pl.when(c >= n_chunks)
                def _(c=c):
                    hi_ref[c * K_CHUNK:(c + 1) * K_CHUNK, :] = lowest
                    lo_ref[c * K_CHUNK:(c + 1) * K_CHUNK, :] = lowest
            search(nc)

    thr = thr_ref[...]
    n_ge = nge_ref[...]

    def count(pred):
        def body(c, acc):
            k0 = chunk_start(c)
            hit = pred(keys_ref[pl.ds(k0, K_CHUNK), :], k0).astype(jnp.int32)
            return acc + jnp.sum(hit.reshape(K_CHUNK // COUNT_ROWS, COUNT_ROWS, Q_TILE), axis=0)

        acc = lax.fori_loop(0, n_chunks, body, jnp.zeros((COUNT_ROWS, Q_TILE), jnp.int32))
        return jnp.sum(acc, axis=0, keepdims=True)

    zero = jnp.zeros((1, Q_TILE), jnp.int32)

    @pl.when(jnp.max(n_ge - k_eff) > 0)
    def _():
        need = k_eff - count(lambda k, _: k > thr)

        def ties_before(limit):
            return count(lambda k, k0: (k == thr) & (k0 + row_iota < limit))

        def grow(it, limit):
            cand = limit + jnp.left_shift(jnp.int32(1), 12 - it)
            return jnp.where(ties_before(cand) <= need, cand, limit)

        limit = lax.fori_loop(0, 13, grow, zero)

        def demote(c, _):
            k0 = chunk_start(c)
            k = keys_ref[pl.ds(k0, K_CHUNK), :]
            keys_ref[pl.ds(k0, K_CHUNK), :] = jnp.where((k == thr) & (k0 + row_iota >= limit), thr - 1, k)
            return 0

        lax.fori_loop(0, n_chunks, demote, 0)

    m_ref[...] = jnp.full(m_ref.shape, _NEG_BIG, _F32)
    acc_ref[...] = jnp.zeros(acc_ref.shape, _F32)

    def attend(c, _):
        k0 = chunk_start(c)
        bias = jnp.where(keys_ref[pl.ds(k0, K_CHUNK), :] >= thr, 0.0, _NEG_BIG)
        ckv = ckv_ref[0, pl.ds(k0, K_CHUNK), :]
        ckvt = ckvt_ref[0, :, pl.ds(k0, K_CHUNK)]
        logit = _dot(ckv, qlt_ref[0, 0]) + jnp.concatenate([bias] * N_HEADS, axis=1)
        m_old = m_ref[...]
        m_new = jnp.maximum(m_old, jnp.max(logit, axis=0, keepdims=True))
        p = jnp.exp2(logit - m_new).astype(_BF16)
        acc_ref[...] = jnp.exp2(m_old - m_new) * acc_ref[...] + _dot(ckvt, p)
        m_ref[...] = m_new
        return 0

    lax.fori_loop(0, n_chunks, attend, 0)

    o_lat_t = (acc_ref[:KV_LATENT, :] / acc_ref[KV_LATENT:KV_LATENT + 1, :]).astype(_BF16)
    out_t = [_dot(wuvt_ref[hd], o_lat_t[:, hd * Q_TILE:(hd + 1) * Q_TILE]) for hd in range(N_HEADS)]
    o_ref[0] = jnp.concatenate(out_t, axis=0).T.astype(_BF16)


def _attn(qlt, qit, wit, ki, ckv, ckvt, wuvt, k_sel):
    batch, _, seq = ckvt.shape
    return pl.pallas_call(
        functools.partial(_attn_kernel, k_sel=k_sel),
        grid=(batch, seq // Q_TILE),
        in_specs=[
            pl.BlockSpec((1, 1, KV_LATENT, N_HEADS * Q_TILE), lambda b, i: (b, i, 0, 0)),
            pl.BlockSpec((1, IDX_HEADS * IDX_DIM, Q_TILE), lambda b, i: (b, 0, i)),
            pl.BlockSpec((1, IDX_HEADS, Q_TILE), lambda b, i: (b, 0, i)),
            pl.BlockSpec((1, seq, IDX_DIM), lambda b, i: (b, 0, 0)),
            pl.BlockSpec((1, seq, KV_LATENT), lambda b, i: (b, 0, 0)),
            pl.BlockSpec((1, KV_ROWS, seq), lambda b, i: (b, 0, 0)),
            _const_spec((N_HEADS, HEAD_DIM, KV_LATENT)),
        ],
        out_specs=pl.BlockSpec((1, Q_TILE, N_HEADS * HEAD_DIM), lambda b, i: (b, i, 0)),
        out_shape=jax.ShapeDtypeStruct((batch, seq, N_HEADS * HEAD_DIM), _BF16),
        scratch_shapes=[pltpu.VMEM((seq, Q_TILE), jnp.int32), pltpu.VMEM((seq, Q_TILE), jnp.int16),
                        pltpu.VMEM((seq, Q_TILE), jnp.int16), pltpu.VMEM((1, Q_TILE), jnp.int32),
                        pltpu.VMEM((1, Q_TILE), jnp.int32), pltpu.VMEM((1, N_HEADS * Q_TILE), _F32),
                        pltpu.VMEM((KV_ROWS, N_HEADS * Q_TILE), _F32)],
        compiler_params=pltpu.CompilerParams(dimension_semantics=("arbitrary", "arbitrary"),
                                             vmem_limit_bytes=VMEM_LIMIT_BYTES),
        name="dsa_attn",
    )(qlt, qit, wit, ki, ckv, ckvt, wuvt)


def _merge_kernel(x_ref, attn_ref, pool_ref, halo_ref, sga_ref, sgb_ref, wba_ref, pw_ref, ps_ref, wbp_ref, wo_ref,
                  o_ref, *, tiles_per_seq):
    tile = pl.program_id(0) % tiles_per_seq
    cur = pool_ref[...]
    halo = jnp.where(tile == 0, 0.0, halo_ref[...])
    ext = jnp.concatenate([halo, cur], axis=0)
    t1 = (tile * TOKEN_TILE + 1 + lax.broadcasted_iota(jnp.int32, (TOKEN_TILE, 1), 0)).astype(_F32)
    mixed = []
    win_sum, width = ext, 1
    for g, w in enumerate(POOL_WINDOWS):
        while width < w:
            win_sum = win_sum + pltpu.roll(win_sum, width, 0)
            width *= 2
        cols = slice(g * POOL_GROUP, (g + 1) * POOL_GROUP)
        pooled = win_sum[POOL_HALO:, cols] / jnp.minimum(t1, float(w)) - cur[:, cols]
        mixed.append(_dot(pooled.astype(_BF16), pw_ref[g]))
    mixed = (jnp.concatenate(mixed, axis=-1) * ps_ref[...]).astype(_BF16)
    merged = (sga_ref[...].astype(_F32) * _dot(attn_ref[...], wba_ref[...])
              + sgb_ref[...].astype(_F32) * _dot(mixed, wbp_ref[...]))
    o_ref[...] = x_ref[...] + _dot(merged.astype(_BF16), wo_ref[...])


def _merge(x, attn, pool, sga, sgb, wba, pw, ps, wbp, wo, seq):
    n = x.shape[0]
    tps = seq // TOKEN_TILE
    halo_blocks = TOKEN_TILE // POOL_HALO

    def flat(width):
        return pl.BlockSpec((TOKEN_TILE, width), lambda i: (i, 0))

    return pl.pallas_call(
        functools.partial(_merge_kernel, tiles_per_seq=tps),
        grid=(n // TOKEN_TILE,),
        in_specs=[
            flat(D_MODEL), flat(N_HEADS * HEAD_DIM), flat(POOL_WIDTH),
            pl.BlockSpec((POOL_HALO, POOL_WIDTH), lambda i: (jnp.maximum(i * halo_blocks - 1, 0), 0)),
            flat(D_MODEL), flat(D_MODEL),
            _const_spec((N_HEADS * HEAD_DIM, D_MODEL)), _const_spec((len(POOL_WINDOWS), POOL_GROUP, POOL_GROUP)),
            _const_spec((1, POOL_WIDTH)), _const_spec((POOL_WIDTH, D_MODEL)), _const_spec((D_MODEL, D_MODEL)),
        ],
        out_specs=flat(D_MODEL),
        out_shape=jax.ShapeDtypeStruct((n, D_MODEL), _F32),
        compiler_params=pltpu.CompilerParams(dimension_semantics=("arbitrary",), vmem_limit_bytes=VMEM_LIMIT_BYTES),
        name="merge_out",
    )(x, attn, pool, pool, sga, sgb, wba, pw, ps, wbp, wo)


def _split_w_in(w):
    cuts, off = {}, 0
    for name, width in (("q", N_HEADS * HEAD_DIM), ("ckv", KV_LATENT), ("qi", IDX_HEADS * IDX_DIM), ("ki", IDX_DIM),
                        ("wi", IDX_HEADS), ("pool", POOL_WIDTH), ("ga", D_MODEL), ("gb", D_MODEL)):
        cuts[name] = w[:, off:off + width]
        off += width
    pad_n = jnp.zeros((w.shape[0], LANES - IDX_DIM), w.dtype)
    wn = jnp.concatenate([cuts["ckv"], cuts["pool"], cuts["ga"], cuts["gb"], cuts["ki"], pad_n], axis=1)
    pad_t = jnp.zeros((w.shape[0], BF16_SUBLANES - IDX_HEADS), w.dtype)
    wt = jnp.concatenate([cuts["q"], cuts["qi"], cuts["wi"], pad_t], axis=1).T
    return wn.astype(_BF16), wt.astype(_BF16)


def kernel(x, norm_ffn1, ffn1_gate, ffn1_up, ffn1_down, norm_mix, w_in, norm_kv, w_uk, w_uv, pool_w, pool_scale,
           w_branch_attn, w_branch_pool, w_out, norm_ffn2, ffn2_gate, ffn2_up, ffn2_down, norm_final):
    batch, seq, _ = x.shape
    depth = norm_ffn1.shape[0]
    assert seq % TOKEN_TILE == 0 and seq % Q_TILE == 0
    k_sel = min(TOPK_MAX, seq // 4)
    n = batch * seq
    bf = lambda a: a.astype(_BF16)
    row = lambda a: a.reshape(1, -1)
    gf = row(norm_final)
    h = x.reshape(n, D_MODEL)
    for i in range(depth):
        h = _ffn(h, row(norm_ffn1[i]), bf(ffn1_gate[i]), bf(ffn1_up[i]), bf(ffn1_down[i]), gf, False)
        wn, wt = _split_w_in(w_in[i])
        wuk_h = bf(jnp.transpose(w_uk[i], (1, 0, 2)))
        wuvt_h = bf(jnp.transpose(w_uv[i], (1, 2, 0)))
        qlt, ckv, ckvt, qit, ki, wit, pool, sga, sgb = _proj(h, row(norm_mix[i]), wn, wt, row(norm_kv[i]), wuk_h,
                                                             batch, seq)
        attn = _attn(qlt, qit, wit, ki.reshape(batch, seq, -1), ckv.reshape(batch, seq, -1), ckvt, wuvt_h, k_sel)
        h = _merge(h, attn.reshape(n, -1), pool, sga, sgb, bf(w_branch_attn[i]), bf(pool_w[i]), row(pool_scale[i]),
                   bf(w_branch_pool[i]), bf(w_out[i]), seq)
        h = _ffn(h, row(norm_ffn2[i]), bf(ffn2_gate[i]), bf(ffn2_up[i]), bf(ffn2_down[i]), gf, i == depth - 1)
    return h.reshape(batch, seq, D_MODEL)
```

```python
import functools

import jax
import jax.numpy as jnp
from jax import lax
from jax.experimental import pallas as pl
from jax.experimental.pallas import tpu as pltpu

D_MODEL = 1024
N_HEADS = 8
HEAD_DIM = 64
KV_LATENT = 128
ATTN_SCALE = HEAD_DIM ** -0.5
LOG2_E = 1.4426950408889634
IDX_HEADS = 8
IDX_DIM = 64
IDX_SCALE = (IDX_HEADS ** -0.5) * (IDX_DIM ** -0.5)
TOPK_MAX = 256
POOL_WINDOWS = (2, 4, 8, 16)
POOL_GROUP = 128
POOL_WIDTH = POOL_GROUP * len(POOL_WINDOWS)
POOL_HALO = 16
D_FF = 2816
EPS = 1e-6

LANES = 128
BF16_SUBLANES = 16
KV_ROWS = KV_LATENT + BF16_SUBLANES
VMEM_LIMIT_BYTES = 56 * 1024 * 1024

TOKEN_TILE = 512
FF_CHUNK = 256
Q_TILE = 256
K_CHUNK = Q_TILE
COUNT_ROWS = 32
SEARCH_STEP = 2
SEQ_PER_STEP = 2

_N_CKV = 0
_N_POOL = _N_CKV + KV_LATENT
_N_GA = _N_POOL + POOL_WIDTH
_N_GB = _N_GA + D_MODEL
_N_KI = _N_GB + D_MODEL
_N_TOTAL = _N_KI + LANES
_T_Q = 0
_T_QI = _T_Q + N_HEADS * HEAD_DIM
_T_WI = _T_QI + IDX_HEADS * IDX_DIM
_T_TOTAL = _T_WI + BF16_SUBLANES

_F32 = jnp.float32
_BF16 = jnp.bfloat16
_INT_MIN = -(2 ** 31)
_HALF = 2 ** 15
_NEG_BIG = float(jnp.finfo(jnp.float32).min)


def _const_spec(shape):
    return pl.BlockSpec(shape, lambda *_: (0,) * len(shape), pipeline_mode=pl.Buffered(1))


def _rms(x, g):
    return x * lax.rsqrt(jnp.mean(x * x, axis=-1, keepdims=True) + EPS) * g


def _dot(a, b):
    return jnp.dot(a, b, preferred_element_type=_F32)


def _ffn_kernel(x_ref, g_ref, wg_ref, wu_ref, wd_ref, gf_ref, o_ref, *, final_norm):
    x = x_ref[...]
    h = _rms(x, g_ref[...]).astype(_BF16)
    acc = jnp.zeros(x.shape, _F32)
    for c in range(D_FF // FF_CHUNK):
        sl = slice(c * FF_CHUNK, (c + 1) * FF_CHUNK)
        gate = _dot(h, wg_ref[:, sl])
        up = _dot(h, wu_ref[:, sl])
        act = (gate * jax.nn.sigmoid(gate) * up).astype(_BF16)
        acc = acc + _dot(act, wd_ref[sl, :])
    y = x + 0.5 * acc
    if final_norm:
        y = _rms(y, gf_ref[...])
    o_ref[...] = y


def _ffn(x, g, wg, wu, wd, gf, final_norm):
    n = x.shape[0]
    tile = pl.BlockSpec((TOKEN_TILE, D_MODEL), lambda i: (i, 0))
    return pl.pallas_call(
        functools.partial(_ffn_kernel, final_norm=final_norm),
        grid=(n // TOKEN_TILE,),
        in_specs=[tile, _const_spec((1, D_MODEL)), _const_spec((D_MODEL, D_FF)), _const_spec((D_MODEL, D_FF)),
                  _const_spec((D_FF, D_MODEL)), _const_spec((1, D_MODEL))],
        out_specs=tile,
        out_shape=jax.ShapeDtypeStruct((n, D_MODEL), _F32),
        compiler_params=pltpu.CompilerParams(dimension_semantics=("arbitrary",), vmem_limit_bytes=VMEM_LIMIT_BYTES),
        name="ffn",
    )(x, g, wg, wu, wd, gf)


def _proj_kernel(x_ref, g_ref, wn_ref, wt_ref, gkv_ref, wuk_ref, qlt_ref, ckv_ref, ckvt_ref, qit_ref, ki_ref, wit_ref,
                 pool_ref, sga_ref, sgb_ref):
    h32 = _rms(x_ref[...], g_ref[...])
    h = h32.astype(_BF16)
    ht = h32.T.astype(_BF16)

    def z(lo, hi):
        return _dot(h, wn_ref[:, lo:hi])

    def zt(lo, hi):
        return _dot(wt_ref[lo:hi, :], ht)

    ckv = _rms(z(_N_CKV, _N_POOL), gkv_ref[...])
    ckv_ref[...] = ckv.astype(_BF16)
    ones_tile = (lax.broadcasted_iota(jnp.int32, (BF16_SUBLANES, TOKEN_TILE), 0) == 0).astype(_F32)
    ckvt_ref[0] = jnp.concatenate([ckv.T, ones_tile], axis=0).astype(_BF16)
    pool_ref[...] = z(_N_POOL, _N_GA)
    sga_ref[...] = jax.nn.sigmoid(z(_N_GA, _N_GB)).astype(_BF16)
    sgb_ref[...] = jax.nn.sigmoid(z(_N_GB, _N_KI)).astype(_BF16)
    ki_ref[...] = z(_N_KI, _N_TOTAL)[:, :IDX_DIM].astype(_BF16)

    qt = zt(_T_Q, _T_QI).astype(_BF16)
    for hd in range(N_HEADS):
        qlt = (_dot(wuk_ref[hd], qt[hd * HEAD_DIM:(hd + 1) * HEAD_DIM, :]) * (ATTN_SCALE * LOG2_E)).astype(_BF16)
        for j in range(TOKEN_TILE // Q_TILE):
            qlt_ref[0, j, :, hd * Q_TILE:(hd + 1) * Q_TILE] = qlt[:, j * Q_TILE:(j + 1) * Q_TILE]
    qit_ref[0] = zt(_T_QI, _T_WI).astype(_BF16)
    wit_ref[0] = zt(_T_WI, _T_TOTAL)[:IDX_HEADS, :] * IDX_SCALE


def _proj(x, g, wn, wt, gkv, wuk, batch, seq):
    n = x.shape[0]
    tps = seq // TOKEN_TILE

    def flat(width):
        return pl.BlockSpec((TOKEN_TILE, width), lambda i: (i, 0))

    def feat(rows):
        return pl.BlockSpec((1, rows, TOKEN_TILE), lambda i: (i // tps, 0, i % tps))

    out_shape = [
        jax.ShapeDtypeStruct((batch, seq // Q_TILE, KV_LATENT, N_HEADS * Q_TILE), _BF16),
        jax.ShapeDtypeStruct((n, KV_LATENT), _BF16),
        jax.ShapeDtypeStruct((batch, KV_ROWS, seq), _BF16),
        jax.ShapeDtypeStruct((batch, IDX_HEADS * IDX_DIM, seq), _BF16),
        jax.ShapeDtypeStruct((n, IDX_DIM), _BF16),
        jax.ShapeDtypeStruct((batch, IDX_HEADS, seq), _F32),
        jax.ShapeDtypeStruct((n, POOL_WIDTH), _F32),
        jax.ShapeDtypeStruct((n, D_MODEL), _BF16),
        jax.ShapeDtypeStruct((n, D_MODEL), _BF16),
    ]
    out_specs = [
        pl.BlockSpec((1, TOKEN_TILE // Q_TILE, KV_LATENT, N_HEADS * Q_TILE), lambda i: (i // tps, i % tps, 0, 0)),
        flat(KV_LATENT), feat(KV_ROWS), feat(IDX_HEADS * IDX_DIM), flat(IDX_DIM), feat(IDX_HEADS),
        flat(POOL_WIDTH), flat(D_MODEL), flat(D_MODEL),
    ]
    return pl.pallas_call(
        _proj_kernel,
        grid=(n // TOKEN_TILE,),
        in_specs=[flat(D_MODEL), _const_spec((1, D_MODEL)), _const_spec((D_MODEL, _N_TOTAL)),
                  _const_spec((_T_TOTAL, D_MODEL)), _const_spec((1, KV_LATENT)),
                  _const_spec((N_HEADS, KV_LATENT, HEAD_DIM))],
        out_specs=out_specs,
        out_shape=out_shape,
        compiler_params=pltpu.CompilerParams(dimension_semantics=("arbitrary",), vmem_limit_bytes=VMEM_LIMIT_BYTES),
        name="mix_proj",
    )(x, g, wn, wt, gkv, wuk)


def _attn_kernel(qlt_ref, qit_ref, wit_ref, ki_ref, ckv_ref, ckvt_ref, wuvt_ref, o_ref, keys_ref, hi_ref, lo_ref,
                 thr_ref, nge_ref, m_ref, acc_ref, *, k_sel):
    qb = pl.program_id(1)
    n_chunks = qb + 1
    q_pos = qb * Q_TILE + lax.broadcasted_iota(jnp.int32, (1, Q_TILE), 1)
    k_eff = jnp.minimum(k_sel, q_pos + 1)
    row_iota = lax.broadcasted_iota(jnp.int32, (K_CHUNK, Q_TILE), 0)
    seqs = range(SEQ_PER_STEP)

    def chunk_start(c):
        return pl.multiple_of(c * K_CHUNK, K_CHUNK)

    def score_chunk(c, _):
        k0 = chunk_start(c)
        for s in seqs:
            kc = ki_ref[s, pl.ds(k0, K_CHUNK), :]
            score = jnp.zeros((K_CHUNK, Q_TILE), _F32)
            for j in range(IDX_HEADS):
                r = _dot(kc, qit_ref[s, j * IDX_DIM:(j + 1) * IDX_DIM, :])
                score = score + jnp.maximum(r, 0.0) * wit_ref[s, j:j + 1, :]
            bits = pltpu.bitcast(score, jnp.int32)
            key = bits ^ ((bits >> 31) & 0x7FFFFFFF)
            key = jnp.where(k0 + row_iota <= q_pos, key, _INT_MIN)
            keys_ref[s, pl.ds(k0, K_CHUNK), :] = key
            hi_ref[s, pl.ds(k0, K_CHUNK), :] = (key >> 16).astype(jnp.int16)
            lo_ref[s, pl.ds(k0, K_CHUNK), :] = ((key & 0xFFFF) - _HALF).astype(jnp.int16)
        return 0

    lax.fori_loop(0, n_chunks, score_chunk, 0)

    def count16_rows(ref, s, rows, pred, acc):
        hit = pred(ref[s, rows, :]).astype(jnp.int16)
        parts = [hit[r * COUNT_ROWS:(r + 1) * COUNT_ROWS, :] for r in range(K_CHUNK // COUNT_ROWS)]
        while len(parts) > 1:
            parts = [x + y for x, y in zip(parts[::2], parts[1::2])]
        return acc + parts[0]

    def total(acc):
        return jnp.sum(acc.astype(jnp.int32), axis=0, keepdims=True)

    zero16 = jnp.zeros((COUNT_ROWS, Q_TILE), jnp.int16)

    def search(nc):
        def count16(ref, s, pred):
            acc = zero16
            for c in range(nc):
                acc = count16_rows(ref, s, slice(c * K_CHUNK, (c + 1) * K_CHUNK), pred, acc)
            return total(acc)

        def kth_largest16(ref, ks):
            init = []
            for s in seqs:
                n_nonneg = count16(ref, s, lambda v: v >= jnp.zeros((1, Q_TILE), jnp.int16))
                sign_ok = n_nonneg >= ks[s]
                init.append(jnp.where(sign_ok, 0, -_HALF))
                init.append(jnp.where(sign_ok, n_nonneg, nc * K_CHUNK))

            def bisect(it, carry):
                out = []
                for s in seqs:
                    best, n_best = carry[2 * s], carry[2 * s + 1]
                    cand = best + jnp.left_shift(jnp.int32(1), 14 - it)
                    cand16 = cand.astype(jnp.int16)
                    n_cand = count16(ref, s, lambda v: v >= cand16)
                    ok = n_cand >= ks[s]
                    out += [jnp.where(ok, cand, best), jnp.where(ok, n_cand, n_best)]
                return tuple(out)

            res = lax.fori_loop(0, 15, bisect, tuple(init))
            return [res[2 * s] for s in seqs], [res[2 * s + 1] for s in seqs]

        hi_thr, n_ge_hi = kth_largest16(hi_ref, [k_eff] * SEQ_PER_STEP)
        n_gt_hi = []
        for s in seqs:
            hi_thr16 = hi_thr[s].astype(jnp.int16)
            n_gt_hi.append(count16(hi_ref, s, lambda v: v > hi_thr16))
            for c in range(nc):
                rows = slice(c * K_CHUNK, (c + 1) * K_CHUNK)
                lo_ref[s, rows, :] = jnp.where(hi_ref[s, rows, :] == hi_thr16, lo_ref[s, rows, :], jnp.int16(-_HALF))
        lo_thr, n_ge_lo = kth_largest16(lo_ref, [k_eff - n for n in n_gt_hi])
        for s in seqs:
            thr_ref[s] = hi_thr[s] * (2 * _HALF) + (lo_thr[s] + _HALF)
            nge_ref[s] = jnp.where(lo_thr[s] > -_HALF, n_gt_hi[s] + n_ge_lo[s], n_ge_hi[s])

    n_max = keys_ref.shape[1] // K_CHUNK
    lowest = jnp.full((K_CHUNK, Q_TILE), -_HALF, jnp.int16)
    for nc in sorted(set(range(SEARCH_STEP, n_max + 1, SEARCH_STEP)) | {n_max}):
        @pl.when((n_chunks > nc - SEARCH_STEP) & (n_chunks <= nc))
        def _(nc=nc):
            for c in range(max(nc - SEARCH_STEP + 1, 1), nc):
                @pl.when(c >= n_chunks)
                def _(c=c):
                    for s in seqs:
                        hi_ref[s, c * K_CHUNK:(c + 1) * K_CHUNK, :] = lowest
                        lo_ref[s, c * K_CHUNK:(c + 1) * K_CHUNK, :] = lowest
            search(nc)

    thr = [thr_ref[s] for s in seqs]
    surplus = [nge_ref[s] - k_eff for s in seqs]

    @pl.when(functools.reduce(jnp.maximum, [jnp.max(x) for x in surplus]) > 0)
    def _():
        def count_chunks(ref, pred):
            def body(c, accs):
                rows = pl.ds(chunk_start(c), K_CHUNK)
                return tuple(count16_rows(ref, s, rows, pred[s], accs[s]) for s in seqs)

            return [total(a) for a in lax.fori_loop(0, n_chunks, body, (zero16,) * SEQ_PER_STEP)]

        def mark(c, _):
            k0 = chunk_start(c)
            for s in seqs:
                k = keys_ref[s, pl.ds(k0, K_CHUNK), :]
                lo_ref[s, pl.ds(k0, K_CHUNK), :] = jnp.where(k == thr[s], k0 + row_iota, _HALF - 1).astype(jnp.int16)
                hi_ref[s, pl.ds(k0, K_CHUNK), :] = (k > thr[s]).astype(jnp.int16)
            return 0

        lax.fori_loop(0, n_chunks, mark, 0)
        one16 = jnp.ones((1, Q_TILE), jnp.int16)
        n_gt = count_chunks(hi_ref, [lambda v: v >= one16] * SEQ_PER_STEP)
        need = [k_eff - n for n in n_gt]
        pos_bits = keys_ref.shape[1].bit_length()

        def grow(it, limits):
            cands = [lim + jnp.left_shift(jnp.int32(1), pos_bits - 1 - it) for lim in limits]
            cands16 = [c.astype(jnp.int16) for c in cands]
            n_before = count_chunks(lo_ref, [lambda v, c16=c16: v < c16 for c16 in cands16])
            return tuple(jnp.where(n_before[s] <= need[s], cands[s], limits[s]) for s in seqs)

        limits = lax.fori_loop(0, pos_bits, grow, (jnp.zeros((1, Q_TILE), jnp.int32),) * SEQ_PER_STEP)

        def demote(c, _):
            k0 = chunk_start(c)
            for s in seqs:
                k = keys_ref[s, pl.ds(k0, K_CHUNK), :]
                drop = (k == thr[s]) & (k0 + row_iota >= limits[s])
                keys_ref[s, pl.ds(k0, K_CHUNK), :] = jnp.where(drop, thr[s] - 1, k)
            return 0

        lax.fori_loop(0, n_chunks, demote, 0)

    m_ref[...] = jnp.full(m_ref.shape, _NEG_BIG, _F32)
    acc_ref[...] = jnp.zeros(acc_ref.shape, _F32)

    def attend(c, _):
        k0 = chunk_start(c)
        for s in seqs:
            bias = jnp.where(keys_ref[s, pl.ds(k0, K_CHUNK), :] >= thr[s], 0.0, _NEG_BIG)
            ckv = ckv_ref[s, pl.ds(k0, K_CHUNK), :]
            ckvt = ckvt_ref[s, :, pl.ds(k0, K_CHUNK)]
            logit = _dot(ckv, qlt_ref[s, 0]) + jnp.concatenate([bias] * N_HEADS, axis=1)
            m_old = m_ref[s]
            m_new = jnp.maximum(m_old, jnp.max(logit, axis=0, keepdims=True))
            p = jnp.exp2(logit - m_new).astype(_BF16)
            acc_ref[s] = jnp.exp2(m_old - m_new) * acc_ref[s] + _dot(ckvt, p)
            m_ref[s] = m_new
        return 0

    lax.fori_loop(0, n_chunks, attend, 0)

    for s in seqs:
        o_lat_t = (acc_ref[s, :KV_LATENT, :] / acc_ref[s, KV_LATENT:KV_LATENT + 1, :]).astype(_BF16)
        out_t = [_dot(wuvt_ref[hd], o_lat_t[:, hd * Q_TILE:(hd + 1) * Q_TILE]) for hd in range(N_HEADS)]
        o_ref[s] = jnp.concatenate(out_t, axis=0).T.astype(_BF16)


def _attn(qlt, qit, wit, ki, ckv, ckvt, wuvt, k_sel):
    batch, _, seq = ckvt.shape
    sp = SEQ_PER_STEP
    return pl.pallas_call(
        functools.partial(_attn_kernel, k_sel=k_sel),
        grid=(batch // sp, seq // Q_TILE),
        in_specs=[
            pl.BlockSpec((sp, 1, KV_LATENT, N_HEADS * Q_TILE), lambda b, i: (b, i, 0, 0)),
            pl.BlockSpec((sp, IDX_HEADS * IDX_DIM, Q_TILE), lambda b, i: (b, 0, i)),
            pl.BlockSpec((sp, IDX_HEADS, Q_TILE), lambda b, i: (b, 0, i)),
            pl.BlockSpec((sp, seq, IDX_DIM), lambda b, i: (b, 0, 0)),
            pl.BlockSpec((sp, seq, KV_LATENT), lambda b, i: (b, 0, 0)),
            pl.BlockSpec((sp, KV_ROWS, seq), lambda b, i: (b, 0, 0)),
            _const_spec((N_HEADS, HEAD_DIM, KV_LATENT)),
        ],
        out_specs=pl.BlockSpec((sp, Q_TILE, N_HEADS * HEAD_DIM), lambda b, i: (b, i, 0)),
        out_shape=jax.ShapeDtypeStruct((batch, seq, N_HEADS * HEAD_DIM), _BF16),
        scratch_shapes=[pltpu.VMEM((sp, seq, Q_TILE), jnp.int32), pltpu.VMEM((sp, seq, Q_TILE), jnp.int16),
                        pltpu.VMEM((sp, seq, Q_TILE), jnp.int16), pltpu.VMEM((sp, 1, Q_TILE), jnp.int32),
                        pltpu.VMEM((sp, 1, Q_TILE), jnp.int32), pltpu.VMEM((sp, 1, N_HEADS * Q_TILE), _F32),
                        pltpu.VMEM((sp, KV_ROWS, N_HEADS * Q_TILE), _F32)],
        compiler_params=pltpu.CompilerParams(dimension_semantics=("arbitrary", "arbitrary"),
                                             vmem_limit_bytes=VMEM_LIMIT_BYTES),
        name="dsa_attn",
    )(qlt, qit, wit, ki, ckv, ckvt, wuvt)


def _merge_kernel(x_ref, attn_ref, pool_ref, halo_ref, sga_ref, sgb_ref, wba_ref, pw_ref, ps_ref, wbp_ref, wo_ref,
                  o_ref, *, tiles_per_seq):
    tile = pl.program_id(0) % tiles_per_seq
    cur = pool_ref[...]
    halo = jnp.where(tile == 0, 0.0, halo_ref[...])
    ext = jnp.concatenate([halo, cur], axis=0)
    t1 = (tile * TOKEN_TILE + 1 + lax.broadcasted_iota(jnp.int32, (TOKEN_TILE, 1), 0)).astype(_F32)
    mixed = []
    win_sum, width = ext, 1
    for g, w in enumerate(POOL_WINDOWS):
        while width < w:
            win_sum = win_sum + pltpu.roll(win_sum, width, 0)
            width *= 2
        cols = slice(g * POOL_GROUP, (g + 1) * POOL_GROUP)
        pooled = win_sum[POOL_HALO:, cols] / jnp.minimum(t1, float(w)) - cur[:, cols]
        mixed.append(_dot(pooled.astype(_BF16), pw_ref[g]))
    mixed = (jnp.concatenate(mixed, axis=-1) * ps_ref[...]).astype(_BF16)
    merged = (sga_ref[...].astype(_F32) * _dot(attn_ref[...], wba_ref[...])
              + sgb_ref[...].astype(_F32) * _dot(mixed, wbp_ref[...]))
    o_ref[...] = x_ref[...] + _dot(merged.astype(_BF16), wo_ref[...])


def _merge(x, attn, pool, sga, sgb, wba, pw, ps, wbp, wo, seq):
    n = x.shape[0]
    tps = seq // TOKEN_TILE
    halo_blocks = TOKEN_TILE // POOL_HALO

    def flat(width):
        return pl.BlockSpec((TOKEN_TILE, width), lambda i: (i, 0))

    return pl.pallas_call(
        functools.partial(_merge_kernel, tiles_per_seq=tps),
        grid=(n // TOKEN_TILE,),
        in_specs=[
            flat(D_MODEL), flat(N_HEADS * HEAD_DIM), flat(POOL_WIDTH),
            pl.BlockSpec((POOL_HALO, POOL_WIDTH), lambda i: (jnp.maximum(i * halo_blocks - 1, 0), 0)),
            flat(D_MODEL), flat(D_MODEL),
            _const_spec((N_HEADS * HEAD_DIM, D_MODEL)), _const_spec((len(POOL_WINDOWS), POOL_GROUP, POOL_GROUP)),
            _const_spec((1, POOL_WIDTH)), _const_spec((POOL_WIDTH, D_MODEL)), _const_spec((D_MODEL, D_MODEL)),
        ],
        out_specs=flat(D_MODEL),
        out_shape=jax.ShapeDtypeStruct((n, D_MODEL), _F32),
        compiler_params=pltpu.CompilerParams(dimension_semantics=("arbitrary",), vmem_limit_bytes=VMEM_LIMIT_BYTES),
        name="merge_out",
    )(x, attn, pool, pool, sga, sgb, wba, pw, ps, wbp, wo)


def _split_w_in(w):
    cuts, off = {}, 0
    for name, width in (("q", N_HEADS * HEAD_DIM), ("ckv", KV_LATENT), ("qi", IDX_HEADS * IDX_DIM), ("ki", IDX_DIM),
                        ("wi", IDX_HEADS), ("pool", POOL_WIDTH), ("ga", D_MODEL), ("gb", D_MODEL)):
        cuts[name] = w[:, off:off + width]
        off += width
    pad_n = jnp.zeros((w.shape[0], LANES - IDX_DIM), w.dtype)
    wn = jnp.concatenate([cuts["ckv"], cuts["pool"], cuts["ga"], cuts["gb"], cuts["ki"], pad_n], axis=1)
    pad_t = jnp.zeros((w.shape[0], BF16_SUBLANES - IDX_HEADS), w.dtype)
    wt = jnp.concatenate([cuts["q"], cuts["qi"], cuts["wi"], pad_t], axis=1).T
    return wn.astype(_BF16), wt.astype(_BF16)


def kernel(x, norm_ffn1, ffn1_gate, ffn1_up, ffn1_down, norm_mix, w_in, norm_kv, w_uk, w_uv, pool_w, pool_scale,
           w_branch_attn, w_branch_pool, w_out, norm_ffn2, ffn2_gate, ffn2_up, ffn2_down, norm_final):
    batch, seq, _ = x.shape
    depth = norm_ffn1.shape[0]
    assert seq % TOKEN_TILE == 0 and seq % Q_TILE == 0 and batch % SEQ_PER_STEP == 0
    assert seq < _HALF
    k_sel = min(TOPK_MAX, seq // 4)
    n = batch * seq
    bf = lambda a: a.astype(_BF16)
    row = lambda a: a.reshape(1, -1)
    gf = row(norm_final)
    h = x.reshape(n, D_MODEL)
    for i in range(depth):
        h = _ffn(h, row(norm_ffn1[i]), bf(ffn1_gate[i]), bf(ffn1_up[i]), bf(ffn1_down[i]), gf, False)
        wn, wt = _split_w_in(w_in[i])
        wuk_h = bf(jnp.transpose(w_uk[i], (1, 0, 2)))
        wuvt_h = bf(jnp.transpose(w_uv[i], (1, 2, 0)))
        qlt, ckv, ckvt, qit, ki, wit, pool, sga, sgb = _proj(h, row(norm_mix[i]), wn, wt, row(norm_kv[i]), wuk_h,
                                                             batch, seq)
        attn = _attn(qlt, qit, wit, ki.reshape(batch, seq, -1), ckv.reshape(batch, seq, -1), ckvt, wuvt_h, k_sel)
        h = _merge(h, attn.reshape(n, -1), pool, sga, sgb, bf(w_branch_attn[i]), bf(pool_w[i]), row(pool_scale[i]),
                   bf(w_branch_pool[i]), bf(w_out[i]), seq)
        h = _ffn(h, row(norm_ffn2[i]), bf(ffn2_gate[i]), bf(ffn2_up[i]), bf(ffn2_down[i]), gf, i == depth - 1)
    return h.reshape(batch, seq, D_MODEL)
```

```python
import functools

import jax
import jax.numpy as jnp
from jax import lax
from jax.experimental import pallas as pl
from jax.experimental.pallas import tpu as pltpu

D_MODEL = 1024
N_HEADS = 8
HEAD_DIM = 64
KV_LATENT = 128
ATTN_SCALE = HEAD_DIM ** -0.5
LOG2_E = 1.4426950408889634
IDX_HEADS = 8
IDX_DIM = 64
IDX_SCALE = (IDX_HEADS ** -0.5) * (IDX_DIM ** -0.5)
TOPK_MAX = 256
POOL_WINDOWS = (2, 4, 8, 16)
POOL_GROUP = 128
POOL_WIDTH = POOL_GROUP * len(POOL_WINDOWS)
POOL_HALO = 16
D_FF = 2816
EPS = 1e-6

LANES = 128
BF16_SUBLANES = 16
KV_ROWS = KV_LATENT + BF16_SUBLANES
VMEM_LIMIT_BYTES = 56 * 1024 * 1024

TOKEN_TILE = 512
FF_CHUNK = 256
Q_TILE = 256
K_CHUNK = Q_TILE
COUNT_ROWS = 32
SEARCH_STEP = 2
SEQ_PER_STEP = 4

_N_CKV = 0
_N_POOL = _N_CKV + KV_LATENT
_N_GA = _N_POOL + POOL_WIDTH
_N_GB = _N_GA + D_MODEL
_N_KI = _N_GB + D_MODEL
_N_TOTAL = _N_KI + LANES
_T_Q = 0
_T_QI = _T_Q + N_HEADS * HEAD_DIM
_T_WI = _T_QI + IDX_HEADS * IDX_DIM
_T_TOTAL = _T_WI + BF16_SUBLANES

_F32 = jnp.float32
_BF16 = jnp.bfloat16
_INT_MIN = -(2 ** 31)
_HALF = 2 ** 15
_NEG_BIG = float(jnp.finfo(jnp.float32).min)


def _const_spec(shape):
    return pl.BlockSpec(shape, lambda *_: (0,) * len(shape), pipeline_mode=pl.Buffered(1))


def _rms(x, g):
    return x * lax.rsqrt(jnp.mean(x * x, axis=-1, keepdims=True) + EPS) * g


def _dot(a, b):
    return jnp.dot(a, b, preferred_element_type=_F32)


def _ffn_kernel(x_ref, g_ref, wg_ref, wu_ref, wd_ref, gf_ref, o_ref, *, final_norm):
    x = x_ref[...]
    h = _rms(x, g_ref[...]).astype(_BF16)
    acc = jnp.zeros(x.shape, _F32)
    for c in range(D_FF // FF_CHUNK):
        sl = slice(c * FF_CHUNK, (c + 1) * FF_CHUNK)
        gate = _dot(h, wg_ref[:, sl])
        up = _dot(h, wu_ref[:, sl])
        act = (gate * jax.nn.sigmoid(gate) * up).astype(_BF16)
        acc = acc + _dot(act, wd_ref[sl, :])
    y = x + 0.5 * acc
    if final_norm:
        y = _rms(y, gf_ref[...])
    o_ref[...] = y


def _ffn(x, g, wg, wu, wd, gf, final_norm):
    n = x.shape[0]
    tile = pl.BlockSpec((TOKEN_TILE, D_MODEL), lambda i: (i, 0))
    return pl.pallas_call(
        functools.partial(_ffn_kernel, final_norm=final_norm),
        grid=(n // TOKEN_TILE,),
        in_specs=[tile, _const_spec((1, D_MODEL)), _const_spec((D_MODEL, D_FF)), _const_spec((D_MODEL, D_FF)),
                  _const_spec((D_FF, D_MODEL)), _const_spec((1, D_MODEL))],
        out_specs=tile,
        out_shape=jax.ShapeDtypeStruct((n, D_MODEL), _F32),
        compiler_params=pltpu.CompilerParams(dimension_semantics=("arbitrary",), vmem_limit_bytes=VMEM_LIMIT_BYTES),
        name="ffn",
    )(x, g, wg, wu, wd, gf)


def _proj_kernel(x_ref, g_ref, wn_ref, wt_ref, gkv_ref, wuk_ref, qlt_ref, ckv_ref, ckvt_ref, qit_ref, ki_ref, wit_ref,
                 pool_ref, sga_ref, sgb_ref):
    h32 = _rms(x_ref[...], g_ref[...])
    h = h32.astype(_BF16)
    ht = h32.T.astype(_BF16)

    def z(lo, hi):
        return _dot(h, wn_ref[:, lo:hi])

    def zt(lo, hi):
        return _dot(wt_ref[lo:hi, :], ht)

    ckv = _rms(z(_N_CKV, _N_POOL), gkv_ref[...])
    ckv_ref[...] = ckv.astype(_BF16)
    ones_tile = (lax.broadcasted_iota(jnp.int32, (BF16_SUBLANES, TOKEN_TILE), 0) == 0).astype(_F32)
    ckvt_ref[0] = jnp.concatenate([ckv.T, ones_tile], axis=0).astype(_BF16)
    pool_ref[...] = z(_N_POOL, _N_GA)
    sga_ref[...] = jax.nn.sigmoid(z(_N_GA, _N_GB)).astype(_BF16)
    sgb_ref[...] = jax.nn.sigmoid(z(_N_GB, _N_KI)).astype(_BF16)
    ki_ref[...] = z(_N_KI, _N_TOTAL)[:, :IDX_DIM].astype(_BF16)

    qt = zt(_T_Q, _T_QI).astype(_BF16)
    for hd in range(N_HEADS):
        qlt = (_dot(wuk_ref[hd], qt[hd * HEAD_DIM:(hd + 1) * HEAD_DIM, :]) * (ATTN_SCALE * LOG2_E)).astype(_BF16)
        for j in range(TOKEN_TILE // Q_TILE):
            qlt_ref[0, j, :, hd * Q_TILE:(hd + 1) * Q_TILE] = qlt[:, j * Q_TILE:(j + 1) * Q_TILE]
    qit_ref[0] = zt(_T_QI, _T_WI).astype(_BF16)
    wit_ref[0] = zt(_T_WI, _T_TOTAL)[:IDX_HEADS, :] * IDX_SCALE


def _proj(x, g, wn, wt, gkv, wuk, batch, seq):
    n = x.shape[0]
    tps = seq // TOKEN_TILE

    def flat(width):
        return pl.BlockSpec((TOKEN_TILE, width), lambda i: (i, 0))

    def feat(rows):
        return pl.BlockSpec((1, rows, TOKEN_TILE), lambda i: (i // tps, 0, i % tps))

    out_shape = [
        jax.ShapeDtypeStruct((batch, seq // Q_TILE, KV_LATENT, N_HEADS * Q_TILE), _BF16),
        jax.ShapeDtypeStruct((n, KV_LATENT), _BF16),
        jax.ShapeDtypeStruct((batch, KV_ROWS, seq), _BF16),
        jax.ShapeDtypeStruct((batch, IDX_HEADS * IDX_DIM, seq), _BF16),
        jax.ShapeDtypeStruct((n, IDX_DIM), _BF16),
        jax.ShapeDtypeStruct((batch, IDX_HEADS, seq), _F32),
        jax.ShapeDtypeStruct((n, POOL_WIDTH), _F32),
        jax.ShapeDtypeStruct((n, D_MODEL), _BF16),
        jax.ShapeDtypeStruct((n, D_MODEL), _BF16),
    ]
    out_specs = [
        pl.BlockSpec((1, TOKEN_TILE // Q_TILE, KV_LATENT, N_HEADS * Q_TILE), lambda i: (i // tps, i % tps, 0, 0)),
        flat(KV_LATENT), feat(KV_ROWS), feat(IDX_HEADS * IDX_DIM), flat(IDX_DIM), feat(IDX_HEADS),
        flat(POOL_WIDTH), flat(D_MODEL), flat(D_MODEL),
    ]
    return pl.pallas_call(
        _proj_kernel,
        grid=(n // TOKEN_TILE,),
        in_specs=[flat(D_MODEL), _const_spec((1, D_MODEL)), _const_spec((D_MODEL, _N_TOTAL)),
                  _const_spec((_T_TOTAL, D_MODEL)), _const_spec((1, KV_LATENT)),
                  _const_spec((N_HEADS, KV_LATENT, HEAD_DIM))],
        out_specs=out_specs,
        out_shape=out_shape,
        compiler_params=pltpu.CompilerParams(dimension_semantics=("arbitrary",), vmem_limit_bytes=VMEM_LIMIT_BYTES),
        name="mix_proj",
    )(x, g, wn, wt, gkv, wuk)


def _attn_kernel(qlt_ref, qit_ref, wit_ref, ki_ref, ckv_ref, ckvt_ref, wuvt_ref, o_ref, keys_ref, hi_ref, lo_ref,
                 thr_ref, nge_ref, m_ref, acc_ref, *, k_sel):
    qb = pl.program_id(1)
    n_chunks = qb + 1
    q_pos = qb * Q_TILE + lax.broadcasted_iota(jnp.int32, (1, Q_TILE), 1)
    k_eff = jnp.minimum(k_sel, q_pos + 1)
    row_iota = lax.broadcasted_iota(jnp.int32, (K_CHUNK, Q_TILE), 0)
    seqs = range(SEQ_PER_STEP)

    def chunk_start(c):
        return pl.multiple_of(c * K_CHUNK, K_CHUNK)

    def score_chunk(c, _):
        k0 = chunk_start(c)
        for s in seqs:
            kc = ki_ref[s, pl.ds(k0, K_CHUNK), :]
            score = jnp.zeros((K_CHUNK, Q_TILE), _F32)
            for j in range(IDX_HEADS):
                r = _dot(kc, qit_ref[s, j * IDX_DIM:(j + 1) * IDX_DIM, :])
                score = score + jnp.maximum(r, 0.0) * wit_ref[s, j:j + 1, :]
            bits = pltpu.bitcast(score, jnp.int32)
            key = bits ^ ((bits >> 31) & 0x7FFFFFFF)
            key = jnp.where(k0 + row_iota <= q_pos, key, _INT_MIN)
            keys_ref[s, pl.ds(k0, K_CHUNK), :] = key
            hi_ref[s, pl.ds(k0, K_CHUNK), :] = (key >> 16).astype(jnp.int16)
            lo_ref[s, pl.ds(k0, K_CHUNK), :] = ((key & 0xFFFF) - _HALF).astype(jnp.int16)
        return 0

    lax.fori_loop(0, n_chunks, score_chunk, 0)

    def count16_rows(ref, s, rows, pred, acc):
        hit = pred(ref[s, rows, :]).astype(jnp.int16)
        parts = [hit[r * COUNT_ROWS:(r + 1) * COUNT_ROWS, :] for r in range(K_CHUNK // COUNT_ROWS)]
        while len(parts) > 1:
            parts = [x + y for x, y in zip(parts[::2], parts[1::2])]
        return acc + parts[0]

    def total(acc):
        return jnp.sum(acc.astype(jnp.int32), axis=0, keepdims=True)

    zero16 = jnp.zeros((COUNT_ROWS, Q_TILE), jnp.int16)

    def search(nc):
        def count16(ref, s, pred):
            acc = zero16
            for c in range(nc):
                acc = count16_rows(ref, s, slice(c * K_CHUNK, (c + 1) * K_CHUNK), pred, acc)
            return total(acc)

        def kth_largest16(ref, ks):
            init = []
            for s in seqs:
                n_nonneg = count16(ref, s, lambda v: v >= jnp.zeros((1, Q_TILE), jnp.int16))
                sign_ok = n_nonneg >= ks[s]
                init.append(jnp.where(sign_ok, 0, -_HALF))
                init.append(jnp.where(sign_ok, n_nonneg, nc * K_CHUNK))

            def bisect(it, carry):
                out = []
                for s in seqs:
                    best, n_best = carry[2 * s], carry[2 * s + 1]
                    cand = best + jnp.left_shift(jnp.int32(1), 14 - it)
                    cand16 = cand.astype(jnp.int16)
                    n_cand = count16(ref, s, lambda v: v >= cand16)
                    ok = n_cand >= ks[s]
                    out += [jnp.where(ok, cand, best), jnp.where(ok, n_cand, n_best)]
                return tuple(out)

            res = lax.fori_loop(0, 15, bisect, tuple(init))
            return [res[2 * s] for s in seqs], [res[2 * s + 1] for s in seqs]

        hi_thr, n_ge_hi = kth_largest16(hi_ref, [k_eff] * SEQ_PER_STEP)
        n_gt_hi = []
        for s in seqs:
            hi_thr16 = hi_thr[s].astype(jnp.int16)
            n_gt_hi.append(count16(hi_ref, s, lambda v: v > hi_thr16))
            for c in range(nc):
                rows = slice(c * K_CHUNK, (c + 1) * K_CHUNK)
                lo_ref[s, rows, :] = jnp.where(hi_ref[s, rows, :] == hi_thr16, lo_ref[s, rows, :], jnp.int16(-_HALF))
        lo_thr, n_ge_lo = kth_largest16(lo_ref, [k_eff - n for n in n_gt_hi])
        for s in seqs:
            thr_ref[s] = hi_thr[s] * (2 * _HALF) + (lo_thr[s] + _HALF)
            nge_ref[s] = jnp.where(lo_thr[s] > -_HALF, n_gt_hi[s] + n_ge_lo[s], n_ge_hi[s])

    n_max = keys_ref.shape[1] // K_CHUNK
    lowest = jnp.full((K_CHUNK, Q_TILE), -_HALF, jnp.int16)
    for nc in sorted(set(range(SEARCH_STEP, n_max + 1, SEARCH_STEP)) | {n_max}):
        @pl.when((n_chunks > nc - SEARCH_STEP) & (n_chunks <= nc))
        def _(nc=nc):
            for c in range(max(nc - SEARCH_STEP + 1, 1), nc):
                @pl.when(c >= n_chunks)
                def _(c=c):
                    for s in seqs:
                        hi_ref[s, c * K_CHUNK:(c + 1) * K_CHUNK, :] = lowest
                        lo_ref[s, c * K_CHUNK:(c + 1) * K_CHUNK, :] = lowest
            search(nc)

    thr = [thr_ref[s] for s in seqs]
    surplus = [nge_ref[s] - k_eff for s in seqs]

    @pl.when(functools.reduce(jnp.maximum, [jnp.max(x) for x in surplus]) > 0)
    def _():
        def count_chunks(ref, pred):
            def body(c, accs):
                rows = pl.ds(chunk_start(c), K_CHUNK)
                return tuple(count16_rows(ref, s, rows, pred[s], accs[s]) for s in seqs)

            return [total(a) for a in lax.fori_loop(0, n_chunks, body, (zero16,) * SEQ_PER_STEP)]

        def mark(c, _):
            k0 = chunk_start(c)
            for s in seqs:
                k = keys_ref[s, pl.ds(k0, K_CHUNK), :]
                lo_ref[s, pl.ds(k0, K_CHUNK), :] = jnp.where(k == thr[s], k0 + row_iota, _HALF - 1).astype(jnp.int16)
                hi_ref[s, pl.ds(k0, K_CHUNK), :] = (k > thr[s]).astype(jnp.int16)
            return 0

        lax.fori_loop(0, n_chunks, mark, 0)
        one16 = jnp.ones((1, Q_TILE), jnp.int16)
        n_gt = count_chunks(hi_ref, [lambda v: v >= one16] * SEQ_PER_STEP)
        need = [k_eff - n for n in n_gt]
        pos_bits = keys_ref.shape[1].bit_length()

        def grow(it, limits):
            cands = [lim + jnp.left_shift(jnp.int32(1), pos_bits - 1 - it) for lim in limits]
            cands16 = [c.astype(jnp.int16) for c in cands]
            n_before = count_chunks(lo_ref, [lambda v, c16=c16: v < c16 for c16 in cands16])
            return tuple(jnp.where(n_before[s] <= need[s], cands[s], limits[s]) for s in seqs)

        limits = lax.fori_loop(0, pos_bits, grow, (jnp.zeros((1, Q_TILE), jnp.int32),) * SEQ_PER_STEP)

        def demote(c, _):
            k0 = chunk_start(c)
            for s in seqs:
                k = keys_ref[s, pl.ds(k0, K_CHUNK), :]
                drop = (k == thr[s]) & (k0 + row_iota >= limits[s])
                keys_ref[s, pl.ds(k0, K_CHUNK), :] = jnp.where(drop, thr[s] - 1, k)
            return 0

        lax.fori_loop(0, n_chunks, demote, 0)

    m_ref[...] = jnp.full(m_ref.shape, _NEG_BIG, _F32)
    acc_ref[...] = jnp.zeros(acc_ref.shape, _F32)

    def attend(c, _):
        k0 = chunk_start(c)
        for s in seqs:
            bias = jnp.where(keys_ref[s, pl.ds(k0, K_CHUNK), :] >= thr[s], 0.0, _NEG_BIG)
            ckv = ckv_ref[s, pl.ds(k0, K_CHUNK), :]
            ckvt = ckvt_ref[s, :, pl.ds(k0, K_CHUNK)]
            logit = _dot(ckv, qlt_ref[s, 0]) + jnp.concatenate([bias] * N_HEADS, axis=1)
            m_old = m_ref[s]
            m_new = jnp.maximum(m_old, jnp.max(logit, axis=0, keepdims=True))
            p = jnp.exp2(logit - m_new).astype(_BF16)
            acc_ref[s] = jnp.exp2(m_old - m_new) * acc_ref[s] + _dot(ckvt, p)
            m_ref[s] = m_new
        return 0

    lax.fori_loop(0, n_chunks, attend, 0)

    for s in seqs:
        o_lat_t = (acc_ref[s, :KV_LATENT, :] / acc_ref[s, KV_LATENT:KV_LATENT + 1, :]).astype(_BF16)
        out_t = [_dot(wuvt_ref[hd], o_lat_t[:, hd * Q_TILE:(hd + 1) * Q_TILE]) for hd in range(N_HEADS)]
        o_ref[s] = jnp.concatenate(out_t, axis=0).T.astype(_BF16)


def _attn(qlt, qit, wit, ki, ckv, ckvt, wuvt, k_sel):
    batch, _, seq = ckvt.shape
    sp = SEQ_PER_STEP
    return pl.pallas_call(
        functools.partial(_attn_kernel, k_sel=k_sel),
        grid=(batch // sp, seq // Q_TILE),
        in_specs=[
            pl.BlockSpec((sp, 1, KV_LATENT, N_HEADS * Q_TILE), lambda b, i: (b, i, 0, 0)),
            pl.BlockSpec((sp, IDX_HEADS * IDX_DIM, Q_TILE), lambda b, i: (b, 0, i)),
            pl.BlockSpec((sp, IDX_HEADS, Q_TILE), lambda b, i: (b, 0, i)),
            pl.BlockSpec((sp, seq, IDX_DIM), lambda b, i: (b, 0, 0)),
            pl.BlockSpec((sp, seq, KV_LATENT), lambda b, i: (b, 0, 0)),
            pl.BlockSpec((sp, KV_ROWS, seq), lambda b, i: (b, 0, 0)),
            _const_spec((N_HEADS, HEAD_DIM, KV_LATENT)),
        ],
        out_specs=pl.BlockSpec((sp, Q_TILE, N_HEADS * HEAD_DIM), lambda b, i: (b, i, 0)),
        out_shape=jax.ShapeDtypeStruct((batch, seq, N_HEADS * HEAD_DIM), _BF16),
        scratch_shapes=[pltpu.VMEM((sp, seq, Q_TILE), jnp.int32), pltpu.VMEM((sp, seq, Q_TILE), jnp.int16),
                        pltpu.VMEM((sp, seq, Q_TILE), jnp.int16), pltpu.VMEM((sp, 1, Q_TILE), jnp.int32),
                        pltpu.VMEM((sp, 1, Q_TILE), jnp.int32), pltpu.VMEM((sp, 1, N_HEADS * Q_TILE), _F32),
                        pltpu.VMEM((sp, KV_ROWS, N_HEADS * Q_TILE), _F32)],
        compiler_params=pltpu.CompilerParams(dimension_semantics=("arbitrary", "arbitrary"),
                                             vmem_limit_bytes=VMEM_LIMIT_BYTES),
        name="dsa_attn",
    )(qlt, qit, wit, ki, ckv, ckvt, wuvt)


def _merge_kernel(x_ref, attn_ref, pool_ref, halo_ref, sga_ref, sgb_ref, wba_ref, pw_ref, ps_ref, wbp_ref, wo_ref,
                  o_ref, *, tiles_per_seq):
    tile = pl.program_id(0) % tiles_per_seq
    cur = pool_ref[...]
    halo = jnp.where(tile == 0, 0.0, halo_ref[...])
    ext = jnp.concatenate([halo, cur], axis=0)
    t1 = (tile * TOKEN_TILE + 1 + lax.broadcasted_iota(jnp.int32, (TOKEN_TILE, 1), 0)).astype(_F32)
    mixed = []
    win_sum, width = ext, 1
    for g, w in enumerate(POOL_WINDOWS):
        while width < w:
            win_sum = win_sum + pltpu.roll(win_sum, width, 0)
            width *= 2
        cols = slice(g * POOL_GROUP, (g + 1) * POOL_GROUP)
        pooled = win_sum[POOL_HALO:, cols] / jnp.minimum(t1, float(w)) - cur[:, cols]
        mixed.append(_dot(pooled.astype(_BF16), pw_ref[g]))
    mixed = (jnp.concatenate(mixed, axis=-1) * ps_ref[...]).astype(_BF16)
    merged = (sga_ref[...].astype(_F32) * _dot(attn_ref[...], wba_ref[...])
              + sgb_ref[...].astype(_F32) * _dot(mixed, wbp_ref[...]))
    o_ref[...] = x_ref[...] + _dot(merged.astype(_BF16), wo_ref[...])


def _merge(x, attn, pool, sga, sgb, wba, pw, ps, wbp, wo, seq):
    n = x.shape[0]
    tps = seq // TOKEN_TILE
    halo_blocks = TOKEN_TILE // POOL_HALO

    def flat(width):
        return pl.BlockSpec((TOKEN_TILE, width), lambda i: (i, 0))

    return pl.pallas_call(
        functools.partial(_merge_kernel, tiles_per_seq=tps),
        grid=(n // TOKEN_TILE,),
        in_specs=[
            flat(D_MODEL), flat(N_HEADS * HEAD_DIM), flat(POOL_WIDTH),
            pl.BlockSpec((POOL_HALO, POOL_WIDTH), lambda i: (jnp.maximum(i * halo_blocks - 1, 0), 0)),
            flat(D_MODEL), flat(D_MODEL),
            _const_spec((N_HEADS * HEAD_DIM, D_MODEL)), _const_spec((len(POOL_WINDOWS), POOL_GROUP, POOL_GROUP)),
            _const_spec((1, POOL_WIDTH)), _const_spec((POOL_WIDTH, D_MODEL)), _const_spec((D_MODEL, D_MODEL)),
        ],
        out_specs=flat(D_MODEL),
        out_shape=jax.ShapeDtypeStruct((n, D_MODEL), _F32),
        compiler_params=pltpu.CompilerParams(dimension_semantics=("arbitrary",), vmem_limit_bytes=VMEM_LIMIT_BYTES),
        name="merge_out",
    )(x, attn, pool, pool, sga, sgb, wba, pw, ps, wbp, wo)


def _split_w_in(w):
    cuts, off = {}, 0
    for name, width in (("q", N_HEADS * HEAD_DIM), ("ckv", KV_LATENT), ("qi", IDX_HEADS * IDX_DIM), ("ki", IDX_DIM),
                        ("wi", IDX_HEADS), ("pool", POOL_WIDTH), ("ga", D_MODEL), ("gb", D_MODEL)):
        cuts[name] = w[:, off:off + width]
        off += width
    pad_n = jnp.zeros((w.shape[0], LANES - IDX_DIM), w.dtype)
    wn = jnp.concatenate([cuts["ckv"], cuts["pool"], cuts["ga"], cuts["gb"], cuts["ki"], pad_n], axis=1)
    pad_t = jnp.zeros((w.shape[0], BF16_SUBLANES - IDX_HEADS), w.dtype)
    wt = jnp.concatenate([cuts["q"], cuts["qi"], cuts["wi"], pad_t], axis=1).T
    return wn.astype(_BF16), wt.astype(_BF16)


def kernel(x, norm_ffn1, ffn1_gate, ffn1_up, ffn1_down, norm_mix, w_in, norm_kv, w_uk, w_uv, pool_w, pool_scale,
           w_branch_attn, w_branch_pool, w_out, norm_ffn2, ffn2_gate, ffn2_up, ffn2_down, norm_final):
    batch, seq, _ = x.shape
    depth = norm_ffn1.shape[0]
    assert seq % TOKEN_TILE == 0 and seq % Q_TILE == 0 and batch % SEQ_PER_STEP == 0
    assert seq < _HALF
    k_sel = min(TOPK_MAX, seq // 4)
    n = batch * seq
    bf = lambda a: a.astype(_BF16)
    row = lambda a: a.reshape(1, -1)
    gf = row(norm_final)
    h = x.reshape(n, D_MODEL)
    for i in range(depth):
        h = _ffn(h, row(norm_ffn1[i]), bf(ffn1_gate[i]), bf(ffn1_up[i]), bf(ffn1_down[i]), gf, False)
        wn, wt = _split_w_in(w_in[i])
        wuk_h = bf(jnp.transpose(w_uk[i], (1, 0, 2)))
        wuvt_h = bf(jnp.transpose(w_uv[i], (1, 2, 0)))
        qlt, ckv, ckvt, qit, ki, wit, pool, sga, sgb = _proj(h, row(norm_mix[i]), wn, wt, row(norm_kv[i]), wuk_h,
                                                             batch, seq)
        attn = _attn(qlt, qit, wit, ki.reshape(batch, seq, -1), ckv.reshape(batch, seq, -1), ckvt, wuvt_h, k_sel)
        h = _merge(h, attn.reshape(n, -1), pool, sga, sgb, bf(w_branch_attn[i]), bf(pool_w[i]), row(pool_scale[i]),
                   bf(w_branch_pool[i]), bf(w_out[i]), seq)
        h = _ffn(h, row(norm_ffn2[i]), bf(ffn2_gate[i]), bf(ffn2_up[i]), bf(ffn2_down[i]), gf, i == depth - 1)
    return h.reshape(batch, seq, D_MODEL)
```

```python
import functools

import jax
import jax.numpy as jnp
from jax import lax
from jax.experimental import pallas as pl
from jax.experimental.pallas import tpu as pltpu

D_MODEL = 1024
N_HEADS = 8
HEAD_DIM = 64
KV_LATENT = 128
ATTN_SCALE = HEAD_DIM ** -0.5
LOG2_E = 1.4426950408889634
IDX_HEADS = 8
IDX_DIM = 64
IDX_SCALE = (IDX_HEADS ** -0.5) * (IDX_DIM ** -0.5)
TOPK_MAX = 256
POOL_WINDOWS = (2, 4, 8, 16)
POOL_GROUP = 128
POOL_WIDTH = POOL_GROUP * len(POOL_WINDOWS)
POOL_HALO = 16
D_FF = 2816
EPS = 1e-6

LANES = 128
BF16_SUBLANES = 16
KV_ROWS = KV_LATENT + BF16_SUBLANES
VMEM_LIMIT_BYTES = 56 * 1024 * 1024

TOKEN_TILE = 512
FF_CHUNK = 256
Q_TILE = 256
K_CHUNK = Q_TILE
COUNT_ROWS = 32
SEARCH_STEP = 2
ATTN_ROWS = 2 * K_CHUNK
SEQ_PER_STEP = 4

_N_CKV = 0
_N_POOL = _N_CKV + KV_LATENT
_N_GA = _N_POOL + POOL_WIDTH
_N_GB = _N_GA + D_MODEL
_N_KI = _N_GB + D_MODEL
_N_TOTAL = _N_KI + LANES
_T_Q = 0
_T_QI = _T_Q + N_HEADS * HEAD_DIM
_T_WI = _T_QI + IDX_HEADS * IDX_DIM
_T_TOTAL = _T_WI + BF16_SUBLANES

_F32 = jnp.float32
_BF16 = jnp.bfloat16
_INT_MIN = -(2 ** 31)
_HALF = 2 ** 15
_NEG_BIG = float(jnp.finfo(jnp.float32).min)


def _const_spec(shape):
    return pl.BlockSpec(shape, lambda *_: (0,) * len(shape), pipeline_mode=pl.Buffered(1))


def _rms(x, g):
    return x * lax.rsqrt(jnp.mean(x * x, axis=-1, keepdims=True) + EPS) * g


def _dot(a, b):
    return jnp.dot(a, b, preferred_element_type=_F32)


def _ffn_kernel(x_ref, g_ref, wg_ref, wu_ref, wd_ref, gf_ref, o_ref, *, final_norm):
    x = x_ref[...]
    h = _rms(x, g_ref[...]).astype(_BF16)
    acc = jnp.zeros(x.shape, _F32)
    for c in range(D_FF // FF_CHUNK):
        sl = slice(c * FF_CHUNK, (c + 1) * FF_CHUNK)
        gate = _dot(h, wg_ref[:, sl])
        up = _dot(h, wu_ref[:, sl])
        act = (gate * jax.nn.sigmoid(gate) * up).astype(_BF16)
        acc = acc + _dot(act, wd_ref[sl, :])
    y = x + 0.5 * acc
    if final_norm:
        y = _rms(y, gf_ref[...])
    o_ref[...] = y


def _ffn(x, g, wg, wu, wd, gf, final_norm):
    n = x.shape[0]
    tile = pl.BlockSpec((TOKEN_TILE, D_MODEL), lambda i: (i, 0))
    return pl.pallas_call(
        functools.partial(_ffn_kernel, final_norm=final_norm),
        grid=(n // TOKEN_TILE,),
        in_specs=[tile, _const_spec((1, D_MODEL)), _const_spec((D_MODEL, D_FF)), _const_spec((D_MODEL, D_FF)),
                  _const_spec((D_FF, D_MODEL)), _const_spec((1, D_MODEL))],
        out_specs=tile,
        out_shape=jax.ShapeDtypeStruct((n, D_MODEL), _F32),
        compiler_params=pltpu.CompilerParams(dimension_semantics=("arbitrary",), vmem_limit_bytes=VMEM_LIMIT_BYTES),
        name="ffn",
    )(x, g, wg, wu, wd, gf)


def _proj_kernel(x_ref, g_ref, wn_ref, wt_ref, gkv_ref, wuk_ref, qlt_ref, ckv_ref, ckvt_ref, qit_ref, ki_ref, wit_ref,
                 pool_ref, sga_ref, sgb_ref):
    h32 = _rms(x_ref[...], g_ref[...])
    h = h32.astype(_BF16)
    ht = h32.T.astype(_BF16)

    def z(lo, hi):
        return _dot(h, wn_ref[:, lo:hi])

    def zt(lo, hi):
        return _dot(wt_ref[lo:hi, :], ht)

    ckv = _rms(z(_N_CKV, _N_POOL), gkv_ref[...])
    ckv_ref[...] = ckv.astype(_BF16)
    ones_tile = (lax.broadcasted_iota(jnp.int32, (BF16_SUBLANES, TOKEN_TILE), 0) == 0).astype(_F32)
    ckvt_ref[0] = jnp.concatenate([ckv.T, ones_tile], axis=0).astype(_BF16)
    pool_ref[...] = z(_N_POOL, _N_GA)
    sga_ref[...] = jax.nn.sigmoid(z(_N_GA, _N_GB)).astype(_BF16)
    sgb_ref[...] = jax.nn.sigmoid(z(_N_GB, _N_KI)).astype(_BF16)
    ki_ref[...] = z(_N_KI, _N_TOTAL)[:, :IDX_DIM].astype(_BF16)

    qt = zt(_T_Q, _T_QI).astype(_BF16)
    for hd in range(N_HEADS):
        qlt = (_dot(wuk_ref[hd], qt[hd * HEAD_DIM:(hd + 1) * HEAD_DIM, :]) * (ATTN_SCALE * LOG2_E)).astype(_BF16)
        for j in range(TOKEN_TILE // Q_TILE):
            qlt_ref[0, j, :, hd * Q_TILE:(hd + 1) * Q_TILE] = qlt[:, j * Q_TILE:(j + 1) * Q_TILE]
    qit_ref[0] = zt(_T_QI, _T_WI).astype(_BF16)
    wit_ref[0] = zt(_T_WI, _T_TOTAL)[:IDX_HEADS, :] * IDX_SCALE


def _proj(x, g, wn, wt, gkv, wuk, batch, seq):
    n = x.shape[0]
    tps = seq // TOKEN_TILE

    def flat(width):
        return pl.BlockSpec((TOKEN_TILE, width), lambda i: (i, 0))

    def feat(rows):
        return pl.BlockSpec((1, rows, TOKEN_TILE), lambda i: (i // tps, 0, i % tps))

    out_shape = [
        jax.ShapeDtypeStruct((batch, seq // Q_TILE, KV_LATENT, N_HEADS * Q_TILE), _BF16),
        jax.ShapeDtypeStruct((n, KV_LATENT), _BF16),
        jax.ShapeDtypeStruct((batch, KV_ROWS, seq), _BF16),
        jax.ShapeDtypeStruct((batch, IDX_HEADS * IDX_DIM, seq), _BF16),
        jax.ShapeDtypeStruct((n, IDX_DIM), _BF16),
        jax.ShapeDtypeStruct((batch, IDX_HEADS, seq), _F32),
        jax.ShapeDtypeStruct((n, POOL_WIDTH), _F32),
        jax.ShapeDtypeStruct((n, D_MODEL), _BF16),
        jax.ShapeDtypeStruct((n, D_MODEL), _BF16),
    ]
    out_specs = [
        pl.BlockSpec((1, TOKEN_TILE // Q_TILE, KV_LATENT, N_HEADS * Q_TILE), lambda i: (i // tps, i % tps, 0, 0)),
        flat(KV_LATENT), feat(KV_ROWS), feat(IDX_HEADS * IDX_DIM), flat(IDX_DIM), feat(IDX_HEADS),
        flat(POOL_WIDTH), flat(D_MODEL), flat(D_MODEL),
    ]
    return pl.pallas_call(
        _proj_kernel,
        grid=(n // TOKEN_TILE,),
        in_specs=[flat(D_MODEL), _const_spec((1, D_MODEL)), _const_spec((D_MODEL, _N_TOTAL)),
                  _const_spec((_T_TOTAL, D_MODEL)), _const_spec((1, KV_LATENT)),
                  _const_spec((N_HEADS, KV_LATENT, HEAD_DIM))],
        out_specs=out_specs,
        out_shape=out_shape,
        compiler_params=pltpu.CompilerParams(dimension_semantics=("arbitrary",), vmem_limit_bytes=VMEM_LIMIT_BYTES),
        name="mix_proj",
    )(x, g, wn, wt, gkv, wuk)


def _attn_kernel(qlt_ref, qit_ref, wit_ref, ki_ref, ckv_ref, ckvt_ref, wuvt_ref, o_ref, keys_ref, hi_ref, lo_ref,
                 thr_ref, nge_ref, m_ref, acc_ref, *, k_sel):
    qb = pl.program_id(1)
    n_chunks = qb + 1
    q_pos = qb * Q_TILE + lax.broadcasted_iota(jnp.int32, (1, Q_TILE), 1)
    k_eff = jnp.minimum(k_sel, q_pos + 1)
    row_iota = lax.broadcasted_iota(jnp.int32, (K_CHUNK, Q_TILE), 0)
    seqs = range(SEQ_PER_STEP)

    def chunk_start(c):
        return pl.multiple_of(c * K_CHUNK, K_CHUNK)

    def score_chunk(c, _):
        k0 = chunk_start(c)
        for s in seqs:
            kc = ki_ref[s, pl.ds(k0, K_CHUNK), :]
            score = jnp.zeros((K_CHUNK, Q_TILE), _F32)
            for j in range(IDX_HEADS):
                r = _dot(kc, qit_ref[s, j * IDX_DIM:(j + 1) * IDX_DIM, :])
                score = score + jnp.maximum(r, 0.0) * wit_ref[s, j:j + 1, :]
            bits = pltpu.bitcast(score, jnp.int32)
            key = bits ^ ((bits >> 31) & 0x7FFFFFFF)
            key = jnp.where(k0 + row_iota <= q_pos, key, _INT_MIN)
            keys_ref[s, pl.ds(k0, K_CHUNK), :] = key
            hi_ref[s, pl.ds(k0, K_CHUNK), :] = (key >> 16).astype(jnp.int16)
            lo_ref[s, pl.ds(k0, K_CHUNK), :] = ((key & 0xFFFF) - _HALF).astype(jnp.int16)
        return 0

    lax.fori_loop(0, n_chunks, score_chunk, 0)

    def count16_rows(ref, s, rows, pred, acc):
        hit = pred(ref[s, rows, :]).astype(jnp.int16)
        parts = [hit[r * COUNT_ROWS:(r + 1) * COUNT_ROWS, :] for r in range(K_CHUNK // COUNT_ROWS)]
        while len(parts) > 1:
            parts = [x + y for x, y in zip(parts[::2], parts[1::2])]
        return acc + parts[0]

    def total(acc):
        return jnp.sum(acc.astype(jnp.int32), axis=0, keepdims=True)

    zero16 = jnp.zeros((COUNT_ROWS, Q_TILE), jnp.int16)

    def search(nc):
        def count16(ref, s, pred):
            acc = zero16
            for c in range(nc):
                acc = count16_rows(ref, s, slice(c * K_CHUNK, (c + 1) * K_CHUNK), pred, acc)
            return total(acc)

        def kth_largest16(ref, ks):
            init = []
            for s in seqs:
                n_nonneg = count16(ref, s, lambda v: v >= jnp.zeros((1, Q_TILE), jnp.int16))
                sign_ok = n_nonneg >= ks[s]
                init.append(jnp.where(sign_ok, 0, -_HALF))
                init.append(jnp.where(sign_ok, n_nonneg, nc * K_CHUNK))

            def bisect(it, carry):
                out = []
                for s in seqs:
                    best, n_best = carry[2 * s], carry[2 * s + 1]
                    cand = best + jnp.left_shift(jnp.int32(1), 14 - it)
                    cand16 = cand.astype(jnp.int16)
                    n_cand = count16(ref, s, lambda v: v >= cand16)
                    ok = n_cand >= ks[s]
                    out += [jnp.where(ok, cand, best), jnp.where(ok, n_cand, n_best)]
                return tuple(out)

            res = lax.fori_loop(0, 15, bisect, tuple(init))
            return [res[2 * s] for s in seqs], [res[2 * s + 1] for s in seqs]

        hi_thr, n_ge_hi = kth_largest16(hi_ref, [k_eff] * SEQ_PER_STEP)
        n_gt_hi = []
        for s in seqs:
            hi_thr16 = hi_thr[s].astype(jnp.int16)
            n_gt_hi.append(count16(hi_ref, s, lambda v: v > hi_thr16))
            for c in range(nc):
                rows = slice(c * K_CHUNK, (c + 1) * K_CHUNK)
                lo_ref[s, rows, :] = jnp.where(hi_ref[s, rows, :] == hi_thr16, lo_ref[s, rows, :], jnp.int16(-_HALF))
        lo_thr, n_ge_lo = kth_largest16(lo_ref, [k_eff - n for n in n_gt_hi])
        for s in seqs:
            thr_ref[s] = hi_thr[s] * (2 * _HALF) + (lo_thr[s] + _HALF)
            nge_ref[s] = jnp.where(lo_thr[s] > -_HALF, n_gt_hi[s] + n_ge_lo[s], n_ge_hi[s])

    n_max = keys_ref.shape[1] // K_CHUNK
    lowest = jnp.full((K_CHUNK, Q_TILE), -_HALF, jnp.int16)
    for nc in sorted(set(range(SEARCH_STEP, n_max + 1, SEARCH_STEP)) | {n_max}):
        @pl.when((n_chunks > nc - SEARCH_STEP) & (n_chunks <= nc))
        def _(nc=nc):
            for c in range(max(nc - SEARCH_STEP + 1, 1), nc):
                @pl.when(c >= n_chunks)
                def _(c=c):
                    for s in seqs:
                        hi_ref[s, c * K_CHUNK:(c + 1) * K_CHUNK, :] = lowest
                        lo_ref[s, c * K_CHUNK:(c + 1) * K_CHUNK, :] = lowest
            search(nc)

    thr = [thr_ref[s] for s in seqs]
    surplus = [nge_ref[s] - k_eff for s in seqs]

    @pl.when(functools.reduce(jnp.maximum, [jnp.max(x) for x in surplus]) > 0)
    def _():
        def count_chunks(ref, pred):
            def body(c, accs):
                rows = pl.ds(chunk_start(c), K_CHUNK)
                return tuple(count16_rows(ref, s, rows, pred[s], accs[s]) for s in seqs)

            return [total(a) for a in lax.fori_loop(0, n_chunks, body, (zero16,) * SEQ_PER_STEP)]

        def mark(c, _):
            k0 = chunk_start(c)
            for s in seqs:
                k = keys_ref[s, pl.ds(k0, K_CHUNK), :]
                lo_ref[s, pl.ds(k0, K_CHUNK), :] = jnp.where(k == thr[s], k0 + row_iota, _HALF - 1).astype(jnp.int16)
                hi_ref[s, pl.ds(k0, K_CHUNK), :] = (k > thr[s]).astype(jnp.int16)
            return 0

        lax.fori_loop(0, n_chunks, mark, 0)
        one16 = jnp.ones((1, Q_TILE), jnp.int16)
        n_gt = count_chunks(hi_ref, [lambda v: v >= one16] * SEQ_PER_STEP)
        need = [k_eff - n for n in n_gt]
        pos_bits = keys_ref.shape[1].bit_length()

        def grow(it, limits):
            cands = [lim + jnp.left_shift(jnp.int32(1), pos_bits - 1 - it) for lim in limits]
            cands16 = [c.astype(jnp.int16) for c in cands]
            n_before = count_chunks(lo_ref, [lambda v, c16=c16: v < c16 for c16 in cands16])
            return tuple(jnp.where(n_before[s] <= need[s], cands[s], limits[s]) for s in seqs)

        limits = lax.fori_loop(0, pos_bits, grow, (jnp.zeros((1, Q_TILE), jnp.int32),) * SEQ_PER_STEP)

        def demote(c, _):
            k0 = chunk_start(c)
            for s in seqs:
                k = keys_ref[s, pl.ds(k0, K_CHUNK), :]
                drop = (k == thr[s]) & (k0 + row_iota >= limits[s])
                keys_ref[s, pl.ds(k0, K_CHUNK), :] = jnp.where(drop, thr[s] - 1, k)
            return 0

        lax.fori_loop(0, n_chunks, demote, 0)

    m_ref[...] = jnp.full(m_ref.shape, _NEG_BIG, _F32)
    acc_ref[...] = jnp.zeros(acc_ref.shape, _F32)

    def attend_rows(k0, size):
        for s in seqs:
            bias = jnp.where(keys_ref[s, pl.ds(k0, size), :] >= thr[s], 0.0, _NEG_BIG)
            ckv = ckv_ref[s, pl.ds(k0, size), :]
            ckvt = ckvt_ref[s, :, pl.ds(k0, size)]
            logit = _dot(ckv, qlt_ref[s, 0]) + jnp.concatenate([bias] * N_HEADS, axis=1)
            m_old = m_ref[s]
            m_new = jnp.maximum(m_old, jnp.max(logit, axis=0, keepdims=True))
            p = jnp.exp2(logit - m_new).astype(_BF16)
            acc_ref[s] = jnp.exp2(m_old - m_new) * acc_ref[s] + _dot(ckvt, p)
            m_ref[s] = m_new

    def attend_wide(c, _):
        attend_rows(pl.multiple_of(c * ATTN_ROWS, ATTN_ROWS), ATTN_ROWS)
        return 0

    per_wide = ATTN_ROWS // K_CHUNK
    lax.fori_loop(0, n_chunks // per_wide, attend_wide, 0)
    for r in range(1, per_wide):
        @pl.when(n_chunks % per_wide >= r)
        def _(r=r):
            attend_rows(chunk_start(n_chunks - n_chunks % per_wide + r - 1), K_CHUNK)

    for s in seqs:
        o_lat_t = (acc_ref[s, :KV_LATENT, :] / acc_ref[s, KV_LATENT:KV_LATENT + 1, :]).astype(_BF16)
        out_t = [_dot(wuvt_ref[hd], o_lat_t[:, hd * Q_TILE:(hd + 1) * Q_TILE]) for hd in range(N_HEADS)]
        o_ref[s] = jnp.concatenate(out_t, axis=0).T.astype(_BF16)


def _attn(qlt, qit, wit, ki, ckv, ckvt, wuvt, k_sel):
    batch, _, seq = ckvt.shape
    sp = SEQ_PER_STEP
    return pl.pallas_call(
        functools.partial(_attn_kernel, k_sel=k_sel),
        grid=(batch // sp, seq // Q_TILE),
        in_specs=[
            pl.BlockSpec((sp, 1, KV_LATENT, N_HEADS * Q_TILE), lambda b, i: (b, i, 0, 0)),
            pl.BlockSpec((sp, IDX_HEADS * IDX_DIM, Q_TILE), lambda b, i: (b, 0, i)),
            pl.BlockSpec((sp, IDX_HEADS, Q_TILE), lambda b, i: (b, 0, i)),
            pl.BlockSpec((sp, seq, IDX_DIM), lambda b, i: (b, 0, 0)),
            pl.BlockSpec((sp, seq, KV_LATENT), lambda b, i: (b, 0, 0)),
            pl.BlockSpec((sp, KV_ROWS, seq), lambda b, i: (b, 0, 0)),
            _const_spec((N_HEADS, HEAD_DIM, KV_LATENT)),
        ],
        out_specs=pl.BlockSpec((sp, Q_TILE, N_HEADS * HEAD_DIM), lambda b, i: (b, i, 0)),
        out_shape=jax.ShapeDtypeStruct((batch, seq, N_HEADS * HEAD_DIM), _BF16),
        scratch_shapes=[pltpu.VMEM((sp, seq, Q_TILE), jnp.int32), pltpu.VMEM((sp, seq, Q_TILE), jnp.int16),
                        pltpu.VMEM((sp, seq, Q_TILE), jnp.int16), pltpu.VMEM((sp, 1, Q_TILE), jnp.int32),
                        pltpu.VMEM((sp, 1, Q_TILE), jnp.int32), pltpu.VMEM((sp, 1, N_HEADS * Q_TILE), _F32),
                        pltpu.VMEM((sp, KV_ROWS, N_HEADS * Q_TILE), _F32)],
        compiler_params=pltpu.CompilerParams(dimension_semantics=("arbitrary", "arbitrary"),
                                             vmem_limit_bytes=VMEM_LIMIT_BYTES),
        name="dsa_attn",
    )(qlt, qit, wit, ki, ckv, ckvt, wuvt)


def _merge_kernel(x_ref, attn_ref, pool_ref, halo_ref, sga_ref, sgb_ref, wba_ref, pw_ref, ps_ref, wbp_ref, wo_ref,
                  o_ref, *, tiles_per_seq):
    tile = pl.program_id(0) % tiles_per_seq
    cur = pool_ref[...]
    halo = jnp.where(tile == 0, 0.0, halo_ref[...])
    ext = jnp.concatenate([halo, cur], axis=0)
    t1 = (tile * TOKEN_TILE + 1 + lax.broadcasted_iota(jnp.int32, (TOKEN_TILE, 1), 0)).astype(_F32)
    mixed = []
    win_sum, width = ext, 1
    for g, w in enumerate(POOL_WINDOWS):
        while width < w:
            win_sum = win_sum + pltpu.roll(win_sum, width, 0)
            width *= 2
        cols = slice(g * POOL_GROUP, (g + 1) * POOL_GROUP)
        pooled = win_sum[POOL_HALO:, cols] / jnp.minimum(t1, float(w)) - cur[:, cols]
        mixed.append(_dot(pooled.astype(_BF16), pw_ref[g]))
    mixed = (jnp.concatenate(mixed, axis=-1) * ps_ref[...]).astype(_BF16)
    merged = (sga_ref[...].astype(_F32) * _dot(attn_ref[...], wba_ref[...])
              + sgb_ref[...].astype(_F32) * _dot(mixed, wbp_ref[...]))
    o_ref[...] = x_ref[...] + _dot(merged.astype(_BF16), wo_ref[...])


def _merge(x, attn, pool, sga, sgb, wba, pw, ps, wbp, wo, seq):
    n = x.shape[0]
    tps = seq // TOKEN_TILE
    halo_blocks = TOKEN_TILE // POOL_HALO

    def flat(width):
        return pl.BlockSpec((TOKEN_TILE, width), lambda i: (i, 0))

    return pl.pallas_call(
        functools.partial(_merge_kernel, tiles_per_seq=tps),
        grid=(n // TOKEN_TILE,),
        in_specs=[
            flat(D_MODEL), flat(N_HEADS * HEAD_DIM), flat(POOL_WIDTH),
            pl.BlockSpec((POOL_HALO, POOL_WIDTH), lambda i: (jnp.maximum(i * halo_blocks - 1, 0), 0)),
            flat(D_MODEL), flat(D_MODEL),
            _const_spec((N_HEADS * HEAD_DIM, D_MODEL)), _const_spec((len(POOL_WINDOWS), POOL_GROUP, POOL_GROUP)),
            _const_spec((1, POOL_WIDTH)), _const_spec((POOL_WIDTH, D_MODEL)), _const_spec((D_MODEL, D_MODEL)),
        ],
        out_specs=flat(D_MODEL),
        out_shape=jax.ShapeDtypeStruct((n, D_MODEL), _F32),
        compiler_params=pltpu.CompilerParams(dimension_semantics=("arbitrary",), vmem_limit_bytes=VMEM_LIMIT_BYTES),
        name="merge_out",
    )(x, attn, pool, pool, sga, sgb, wba, pw, ps, wbp, wo)


def _split_w_in(w):
    cuts, off = {}, 0
    for name, width in (("q", N_HEADS * HEAD_DIM), ("ckv", KV_LATENT), ("qi", IDX_HEADS * IDX_DIM), ("ki", IDX_DIM),
                        ("wi", IDX_HEADS), ("pool", POOL_WIDTH), ("ga", D_MODEL), ("gb", D_MODEL)):
        cuts[name] = w[:, off:off + width]
        off += width
    pad_n = jnp.zeros((w.shape[0], LANES - IDX_DIM), w.dtype)
    wn = jnp.concatenate([cuts["ckv"], cuts["pool"], cuts["ga"], cuts["gb"], cuts["ki"], pad_n], axis=1)
    pad_t = jnp.zeros((w.shape[0], BF16_SUBLANES - IDX_HEADS), w.dtype)
    wt = jnp.concatenate([cuts["q"], cuts["qi"], cuts["wi"], pad_t], axis=1).T
    return wn.astype(_BF16), wt.astype(_BF16)


def kernel(x, norm_ffn1, ffn1_gate, ffn1_up, ffn1_down, norm_mix, w_in, norm_kv, w_uk, w_uv, pool_w, pool_scale,
           w_branch_attn, w_branch_pool, w_out, norm_ffn2, ffn2_gate, ffn2_up, ffn2_down, norm_final):
    batch, seq, _ = x.shape
    depth = norm_ffn1.shape[0]
    assert seq % TOKEN_TILE == 0 and seq % Q_TILE == 0 and batch % SEQ_PER_STEP == 0
    assert seq < _HALF
    k_sel = min(TOPK_MAX, seq // 4)
    n = batch * seq
    bf = lambda a: a.astype(_BF16)
    row = lambda a: a.reshape(1, -1)
    gf = row(norm_final)
    h = x.reshape(n, D_MODEL)
    for i in range(depth):
        h = _ffn(h, row(norm_ffn1[i]), bf(ffn1_gate[i]), bf(ffn1_up[i]), bf(ffn1_down[i]), gf, False)
        wn, wt = _split_w_in(w_in[i])
        wuk_h = bf(jnp.transpose(w_uk[i], (1, 0, 2)))
        wuvt_h = bf(jnp.transpose(w_uv[i], (1, 2, 0)))
        qlt, ckv, ckvt, qit, ki, wit, pool, sga, sgb = _proj(h, row(norm_mix[i]), wn, wt, row(norm_kv[i]), wuk_h,
                                                             batch, seq)
        attn = _attn(qlt, qit, wit, ki.reshape(batch, seq, -1), ckv.reshape(batch, seq, -1), ckvt, wuvt_h, k_sel)
        h = _merge(h, attn.reshape(n, -1), pool, sga, sgb, bf(w_branch_attn[i]), bf(pool_w[i]), row(pool_scale[i]),
                   bf(w_branch_pool[i]), bf(w_out[i]), seq)
        h = _ffn(h, row(norm_ffn2[i]), bf(ffn2_gate[i]), bf(ffn2_up[i]), bf(ffn2_down[i]), gf, i == depth - 1)
    return h.reshape(batch, seq, D_MODEL)
```

```python
import functools

import jax
import jax.numpy as jnp
from jax import lax
from jax.experimental import pallas as pl
from jax.experimental.pallas import tpu as pltpu

D_MODEL = 1024
N_HEADS = 8
HEAD_DIM = 64
KV_LATENT = 128
ATTN_SCALE = HEAD_DIM ** -0.5
LOG2_E = 1.4426950408889634
IDX_HEADS = 8
IDX_DIM = 64
IDX_SCALE = (IDX_HEADS ** -0.5) * (IDX_DIM ** -0.5)
TOPK_MAX = 256
POOL_WINDOWS = (2, 4, 8, 16)
POOL_GROUP = 128
POOL_WIDTH = POOL_GROUP * len(POOL_WINDOWS)
POOL_HALO = 16
D_FF = 2816
EPS = 1e-6

LANES = 128
BF16_SUBLANES = 16
KV_ROWS = KV_LATENT + BF16_SUBLANES
VMEM_LIMIT_BYTES = 56 * 1024 * 1024

TOKEN_TILE = 512
FF_CHUNK = 256
Q_TILE = 256
K_CHUNK = Q_TILE
COUNT_ROWS = 32
SEARCH_STEP = 1
ATTN_ROWS = 2 * K_CHUNK
SEQ_PER_STEP = 4

_N_CKV = 0
_N_POOL = _N_CKV + KV_LATENT
_N_GA = _N_POOL + POOL_WIDTH
_N_GB = _N_GA + D_MODEL
_N_KI = _N_GB + D_MODEL
_N_TOTAL = _N_KI + LANES
_T_Q = 0
_T_QI = _T_Q + N_HEADS * HEAD_DIM
_T_WI = _T_QI + IDX_HEADS * IDX_DIM
_T_TOTAL = _T_WI + BF16_SUBLANES

_F32 = jnp.float32
_BF16 = jnp.bfloat16
_INT_MIN = -(2 ** 31)
_HALF = 2 ** 15
_NEG_BIG = float(jnp.finfo(jnp.float32).min)


def _const_spec(shape):
    return pl.BlockSpec(shape, lambda *_: (0,) * len(shape), pipeline_mode=pl.Buffered(1))


def _rms(x, g):
    return x * lax.rsqrt(jnp.mean(x * x, axis=-1, keepdims=True) + EPS) * g


def _dot(a, b):
    return jnp.dot(a, b, preferred_element_type=_F32)


def _ffn_kernel(x_ref, g_ref, wg_ref, wu_ref, wd_ref, gf_ref, o_ref, *, final_norm):
    x = x_ref[...]
    h = _rms(x, g_ref[...]).astype(_BF16)
    acc = jnp.zeros(x.shape, _F32)
    for c in range(D_FF // FF_CHUNK):
        sl = slice(c * FF_CHUNK, (c + 1) * FF_CHUNK)
        gate = _dot(h, wg_ref[:, sl])
        up = _dot(h, wu_ref[:, sl])
        act = (gate * jax.nn.sigmoid(gate) * up).astype(_BF16)
        acc = acc + _dot(act, wd_ref[sl, :])
    y = x + 0.5 * acc
    if final_norm:
        y = _rms(y, gf_ref[...])
    o_ref[...] = y


def _ffn(x, g, wg, wu, wd, gf, final_norm):
    n = x.shape[0]
    tile = pl.BlockSpec((TOKEN_TILE, D_MODEL), lambda i: (i, 0))
    return pl.pallas_call(
        functools.partial(_ffn_kernel, final_norm=final_norm),
        grid=(n // TOKEN_TILE,),
        in_specs=[tile, _const_spec((1, D_MODEL)), _const_spec((D_MODEL, D_FF)), _const_spec((D_MODEL, D_FF)),
                  _const_spec((D_FF, D_MODEL)), _const_spec((1, D_MODEL))],
        out_specs=tile,
        out_shape=jax.ShapeDtypeStruct((n, D_MODEL), _F32),
        compiler_params=pltpu.CompilerParams(dimension_semantics=("arbitrary",), vmem_limit_bytes=VMEM_LIMIT_BYTES),
        name="ffn",
    )(x, g, wg, wu, wd, gf)


def _proj_kernel(x_ref, g_ref, wn_ref, wt_ref, gkv_ref, wuk_ref, qlt_ref, ckv_ref, ckvt_ref, qit_ref, ki_ref, wit_ref,
                 pool_ref, sga_ref, sgb_ref):
    h32 = _rms(x_ref[...], g_ref[...])
    h = h32.astype(_BF16)
    ht = h32.T.astype(_BF16)

    def z(lo, hi):
        return _dot(h, wn_ref[:, lo:hi])

    def zt(lo, hi):
        return _dot(wt_ref[lo:hi, :], ht)

    ckv = _rms(z(_N_CKV, _N_POOL), gkv_ref[...])
    ckv_ref[...] = ckv.astype(_BF16)
    ones_tile = (lax.broadcasted_iota(jnp.int32, (BF16_SUBLANES, TOKEN_TILE), 0) == 0).astype(_F32)
    ckvt_ref[0] = jnp.concatenate([ckv.T, ones_tile], axis=0).astype(_BF16)
    pool_ref[...] = z(_N_POOL, _N_GA)
    sga_ref[...] = jax.nn.sigmoid(z(_N_GA, _N_GB)).astype(_BF16)
    sgb_ref[...] = jax.nn.sigmoid(z(_N_GB, _N_KI)).astype(_BF16)
    ki_ref[...] = z(_N_KI, _N_TOTAL)[:, :IDX_DIM].astype(_BF16)

    qt = zt(_T_Q, _T_QI).astype(_BF16)
    for hd in range(N_HEADS):
        qlt = (_dot(wuk_ref[hd], qt[hd * HEAD_DIM:(hd + 1) * HEAD_DIM, :]) * (ATTN_SCALE * LOG2_E)).astype(_BF16)
        for j in range(TOKEN_TILE // Q_TILE):
            qlt_ref[0, j, :, hd * Q_TILE:(hd + 1) * Q_TILE] = qlt[:, j * Q_TILE:(j + 1) * Q_TILE]
    qit_ref[0] = zt(_T_QI, _T_WI).astype(_BF16)
    wit_ref[0] = zt(_T_WI, _T_TOTAL)[:IDX_HEADS, :] * IDX_SCALE


def _proj(x, g, wn, wt, gkv, wuk, batch, seq):
    n = x.shape[0]
    tps = seq // TOKEN_TILE

    def flat(width):
        return pl.BlockSpec((TOKEN_TILE, width), lambda i: (i, 0))

    def feat(rows):
        return pl.BlockSpec((1, rows, TOKEN_TILE), lambda i: (i // tps, 0, i % tps))

    out_shape = [
        jax.ShapeDtypeStruct((batch, seq // Q_TILE, KV_LATENT, N_HEADS * Q_TILE), _BF16),
        jax.ShapeDtypeStruct((n, KV_LATENT), _BF16),
        jax.ShapeDtypeStruct((batch, KV_ROWS, seq), _BF16),
        jax.ShapeDtypeStruct((batch, IDX_HEADS * IDX_DIM, seq), _BF16),
        jax.ShapeDtypeStruct((n, IDX_DIM), _BF16),
        jax.ShapeDtypeStruct((batch, IDX_HEADS, seq), _F32),
        jax.ShapeDtypeStruct((n, POOL_WIDTH), _F32),
        jax.ShapeDtypeStruct((n, D_MODEL), _BF16),
        jax.ShapeDtypeStruct((n, D_MODEL), _BF16),
    ]
    out_specs = [
        pl.BlockSpec((1, TOKEN_TILE // Q_TILE, KV_LATENT, N_HEADS * Q_TILE), lambda i: (i // tps, i % tps, 0, 0)),
        flat(KV_LATENT), feat(KV_ROWS), feat(IDX_HEADS * IDX_DIM), flat(IDX_DIM), feat(IDX_HEADS),
        flat(POOL_WIDTH), flat(D_MODEL), flat(D_MODEL),
    ]
    return pl.pallas_call(
        _proj_kernel,
        grid=(n // TOKEN_TILE,),
        in_specs=[flat(D_MODEL), _const_spec((1, D_MODEL)), _const_spec((D_MODEL, _N_TOTAL)),
                  _const_spec((_T_TOTAL, D_MODEL)), _const_spec((1, KV_LATENT)),
                  _const_spec((N_HEADS, KV_LATENT, HEAD_DIM))],
        out_specs=out_specs,
        out_shape=out_shape,
        compiler_params=pltpu.CompilerParams(dimension_semantics=("arbitrary",), vmem_limit_bytes=VMEM_LIMIT_BYTES),
        name="mix_proj",
    )(x, g, wn, wt, gkv, wuk)


def _attn_kernel(qlt_ref, qit_ref, wit_ref, ki_ref, ckv_ref, ckvt_ref, wuvt_ref, o_ref, keys_ref, hi_ref, lo_ref,
                 thr_ref, nge_ref, m_ref, acc_ref, *, k_sel):
    qb = pl.program_id(1)
    n_chunks = qb + 1
    q_pos = qb * Q_TILE + lax.broadcasted_iota(jnp.int32, (1, Q_TILE), 1)
    k_eff = jnp.minimum(k_sel, q_pos + 1)
    row_iota = lax.broadcasted_iota(jnp.int32, (K_CHUNK, Q_TILE), 0)
    seqs = range(SEQ_PER_STEP)

    def chunk_start(c):
        return pl.multiple_of(c * K_CHUNK, K_CHUNK)

    def score_chunk(c, diagonal):
        k0 = chunk_start(c)
        for s in seqs:
            kc = ki_ref[s, pl.ds(k0, K_CHUNK), :]
            score = jnp.zeros((K_CHUNK, Q_TILE), _F32)
            for j in range(IDX_HEADS):
                r = _dot(kc, qit_ref[s, j * IDX_DIM:(j + 1) * IDX_DIM, :])
                score = score + jnp.maximum(r, 0.0) * wit_ref[s, j:j + 1, :]
            bits = pltpu.bitcast(score, jnp.int32)
            key = bits ^ ((bits >> 31) & 0x7FFFFFFF)
            if diagonal:
                key = jnp.where(k0 + row_iota <= q_pos, key, _INT_MIN)
            keys_ref[s, pl.ds(k0, K_CHUNK), :] = key
            hi_ref[s, pl.ds(k0, K_CHUNK), :] = (key >> 16).astype(jnp.int16)
            lo_ref[s, pl.ds(k0, K_CHUNK), :] = ((key & 0xFFFF) - _HALF).astype(jnp.int16)

    def score_full_chunk(c, _):
        score_chunk(c, False)
        return 0

    lax.fori_loop(0, qb, score_full_chunk, 0)
    score_chunk(qb, True)

    def count16_rows(ref, s, rows, pred, acc):
        hit = pred(ref[s, rows, :]).astype(jnp.int16)
        parts = [hit[r * COUNT_ROWS:(r + 1) * COUNT_ROWS, :] for r in range(K_CHUNK // COUNT_ROWS)]
        while len(parts) > 1:
            parts = [x + y for x, y in zip(parts[::2], parts[1::2])]
        return acc + parts[0]

    def total(acc):
        return jnp.sum(acc.astype(jnp.int32), axis=0, keepdims=True)

    zero16 = jnp.zeros((COUNT_ROWS, Q_TILE), jnp.int16)

    def search(nc):
        def count16(ref, s, pred):
            acc = zero16
            for c in range(nc):
                acc = count16_rows(ref, s, slice(c * K_CHUNK, (c + 1) * K_CHUNK), pred, acc)
            return total(acc)

        def kth_largest16(ref, ks):
            init = []
            for s in seqs:
                n_nonneg = count16(ref, s, lambda v: v >= jnp.zeros((1, Q_TILE), jnp.int16))
                sign_ok = n_nonneg >= ks[s]
                init.append(jnp.where(sign_ok, 0, -_HALF))
                init.append(jnp.where(sign_ok, n_nonneg, nc * K_CHUNK))

            def bisect(it, carry):
                out = []
                for s in seqs:
                    best, n_best = carry[2 * s], carry[2 * s + 1]
                    cand = best + jnp.left_shift(jnp.int32(1), 14 - it)
                    cand16 = cand.astype(jnp.int16)
                    n_cand = count16(ref, s, lambda v: v >= cand16)
                    ok = n_cand >= ks[s]
                    out += [jnp.where(ok, cand, best), jnp.where(ok, n_cand, n_best)]
                return tuple(out)

            res = lax.fori_loop(0, 15, bisect, tuple(init))
            return [res[2 * s] for s in seqs], [res[2 * s + 1] for s in seqs]

        hi_thr, n_ge_hi = kth_largest16(hi_ref, [k_eff] * SEQ_PER_STEP)
        n_gt_hi = []
        for s in seqs:
            hi_thr16 = hi_thr[s].astype(jnp.int16)
            n_gt_hi.append(count16(hi_ref, s, lambda v: v > hi_thr16))
            for c in range(nc):
                rows = slice(c * K_CHUNK, (c + 1) * K_CHUNK)
                lo_ref[s, rows, :] = jnp.where(hi_ref[s, rows, :] == hi_thr16, lo_ref[s, rows, :], jnp.int16(-_HALF))
        lo_thr, n_ge_lo = kth_largest16(lo_ref, [k_eff - n for n in n_gt_hi])
        for s in seqs:
            thr_ref[s] = hi_thr[s] * (2 * _HALF) + (lo_thr[s] + _HALF)
            nge_ref[s] = jnp.where(lo_thr[s] > -_HALF, n_gt_hi[s] + n_ge_lo[s], n_ge_hi[s])

    n_max = keys_ref.shape[1] // K_CHUNK
    lowest = jnp.full((K_CHUNK, Q_TILE), -_HALF, jnp.int16)
    for nc in sorted(set(range(SEARCH_STEP, n_max + 1, SEARCH_STEP)) | {n_max}):
        @pl.when((n_chunks > nc - SEARCH_STEP) & (n_chunks <= nc))
        def _(nc=nc):
            for c in range(max(nc - SEARCH_STEP + 1, 1), nc):
                @pl.when(c >= n_chunks)
                def _(c=c):
                    for s in seqs:
                        hi_ref[s, c * K_CHUNK:(c + 1) * K_CHUNK, :] = lowest
                        lo_ref[s, c * K_CHUNK:(c + 1) * K_CHUNK, :] = lowest
            if nc * K_CHUNK <= k_sel:
                for s in seqs:
                    thr_ref[s] = jnp.full((1, Q_TILE), _INT_MIN + 1, jnp.int32)
                    nge_ref[s] = k_eff
            else:
                search(nc)

    thr = [thr_ref[s] for s in seqs]
    surplus = [nge_ref[s] - k_eff for s in seqs]

    @pl.when(functools.reduce(jnp.maximum, [jnp.max(x) for x in surplus]) > 0)
    def _():
        def count_chunks(ref, pred):
            def body(c, accs):
                rows = pl.ds(chunk_start(c), K_CHUNK)
                return tuple(count16_rows(ref, s, rows, pred[s], accs[s]) for s in seqs)

            return [total(a) for a in lax.fori_loop(0, n_chunks, body, (zero16,) * SEQ_PER_STEP)]

        def mark(c, _):
            k0 = chunk_start(c)
            for s in seqs:
                k = keys_ref[s, pl.ds(k0, K_CHUNK), :]
                lo_ref[s, pl.ds(k0, K_CHUNK), :] = jnp.where(k == thr[s], k0 + row_iota, _HALF - 1).astype(jnp.int16)
                hi_ref[s, pl.ds(k0, K_CHUNK), :] = (k > thr[s]).astype(jnp.int16)
            return 0

        lax.fori_loop(0, n_chunks, mark, 0)
        one16 = jnp.ones((1, Q_TILE), jnp.int16)
        n_gt = count_chunks(hi_ref, [lambda v: v >= one16] * SEQ_PER_STEP)
        need = [k_eff - n for n in n_gt]
        pos_bits = keys_ref.shape[1].bit_length()

        def grow(it, limits):
            cands = [lim + jnp.left_shift(jnp.int32(1), pos_bits - 1 - it) for lim in limits]
            cands16 = [c.astype(jnp.int16) for c in cands]
            n_before = count_chunks(lo_ref, [lambda v, c16=c16: v < c16 for c16 in cands16])
            return tuple(jnp.where(n_before[s] <= need[s], cands[s], limits[s]) for s in seqs)

        limits = lax.fori_loop(0, pos_bits, grow, (jnp.zeros((1, Q_TILE), jnp.int32),) * SEQ_PER_STEP)

        def demote(c, _):
            k0 = chunk_start(c)
            for s in seqs:
                k = keys_ref[s, pl.ds(k0, K_CHUNK), :]
                drop = (k == thr[s]) & (k0 + row_iota >= limits[s])
                keys_ref[s, pl.ds(k0, K_CHUNK), :] = jnp.where(drop, thr[s] - 1, k)
            return 0

        lax.fori_loop(0, n_chunks, demote, 0)

    m_ref[...] = jnp.full(m_ref.shape, _NEG_BIG, _F32)
    acc_ref[...] = jnp.zeros(acc_ref.shape, _F32)

    def attend_rows(k0, size):
        for s in seqs:
            bias = jnp.where(keys_ref[s, pl.ds(k0, size), :] >= thr[s], 0.0, _NEG_BIG)
            ckv = ckv_ref[s, pl.ds(k0, size), :]
            ckvt = ckvt_ref[s, :, pl.ds(k0, size)]
            logit = _dot(ckv, qlt_ref[s, 0]) + jnp.concatenate([bias] * N_HEADS, axis=1)
            m_old = m_ref[s]
            m_new = jnp.maximum(m_old, jnp.max(logit, axis=0, keepdims=True))
            p = jnp.exp2(logit - m_new).astype(_BF16)
            acc_ref[s] = jnp.exp2(m_old - m_new) * acc_ref[s] + _dot(ckvt, p)
            m_ref[s] = m_new

    def attend_wide(c, _):
        attend_rows(pl.multiple_of(c * ATTN_ROWS, ATTN_ROWS), ATTN_ROWS)
        return 0

    per_wide = ATTN_ROWS // K_CHUNK
    lax.fori_loop(0, n_chunks // per_wide, attend_wide, 0)
    for r in range(1, per_wide):
        @pl.when(n_chunks % per_wide >= r)
        def _(r=r):
            attend_rows(chunk_start(n_chunks - n_chunks % per_wide + r - 1), K_CHUNK)

    for s in seqs:
        o_lat_t = (acc_ref[s, :KV_LATENT, :] / acc_ref[s, KV_LATENT:KV_LATENT + 1, :]).astype(_BF16)
        out_t = [_dot(wuvt_ref[hd], o_lat_t[:, hd * Q_TILE:(hd + 1) * Q_TILE]) for hd in range(N_HEADS)]
        o_ref[s] = jnp.concatenate(out_t, axis=0).T.astype(_BF16)


def _attn(qlt, qit, wit, ki, ckv, ckvt, wuvt, k_sel):
    batch, _, seq = ckvt.shape
    sp = SEQ_PER_STEP
    return pl.pallas_call(
        functools.partial(_attn_kernel, k_sel=k_sel),
        grid=(batch // sp, seq // Q_TILE),
        in_specs=[
            pl.BlockSpec((sp, 1, KV_LATENT, N_HEADS * Q_TILE), lambda b, i: (b, i, 0, 0)),
            pl.BlockSpec((sp, IDX_HEADS * IDX_DIM, Q_TILE), lambda b, i: (b, 0, i)),
            pl.BlockSpec((sp, IDX_HEADS, Q_TILE), lambda b, i: (b, 0, i)),
            pl.BlockSpec((sp, seq, IDX_DIM), lambda b, i: (b, 0, 0)),
            pl.BlockSpec((sp, seq, KV_LATENT), lambda b, i: (b, 0, 0)),
            pl.BlockSpec((sp, KV_ROWS, seq), lambda b, i: (b, 0, 0)),
            _const_spec((N_HEADS, HEAD_DIM, KV_LATENT)),
        ],
        out_specs=pl.BlockSpec((sp, Q_TILE, N_HEADS * HEAD_DIM), lambda b, i: (b, i, 0)),
        out_shape=jax.ShapeDtypeStruct((batch, seq, N_HEADS * HEAD_DIM), _BF16),
        scratch_shapes=[pltpu.VMEM((sp, seq, Q_TILE), jnp.int32), pltpu.VMEM((sp, seq, Q_TILE), jnp.int16),
                        pltpu.VMEM((sp, seq, Q_TILE), jnp.int16), pltpu.VMEM((sp, 1, Q_TILE), jnp.int32),
                        pltpu.VMEM((sp, 1, Q_TILE), jnp.int32), pltpu.VMEM((sp, 1, N_HEADS * Q_TILE), _F32),
                        pltpu.VMEM((sp, KV_ROWS, N_HEADS * Q_TILE), _F32)],
        compiler_params=pltpu.CompilerParams(dimension_semantics=("arbitrary", "arbitrary"),
                                             vmem_limit_bytes=VMEM_LIMIT_BYTES),
        name="dsa_attn",
    )(qlt, qit, wit, ki, ckv, ckvt, wuvt)


def _merge_kernel(x_ref, attn_ref, pool_ref, halo_ref, sga_ref, sgb_ref, wba_ref, pw_ref, ps_ref, wbp_ref, wo_ref,
                  o_ref, *, tiles_per_seq):
    tile = pl.program_id(0) % tiles_per_seq
    cur = pool_ref[...]
    halo = jnp.where(tile == 0, 0.0, halo_ref[...])
    ext = jnp.concatenate([halo, cur], axis=0)
    t1 = (tile * TOKEN_TILE + 1 + lax.broadcasted_iota(jnp.int32, (TOKEN_TILE, 1), 0)).astype(_F32)
    mixed = []
    win_sum, width = ext, 1
    for g, w in enumerate(POOL_WINDOWS):
        while width < w:
            win_sum = win_sum + pltpu.roll(win_sum, width, 0)
            width *= 2
        cols = slice(g * POOL_GROUP, (g + 1) * POOL_GROUP)
        pooled = win_sum[POOL_HALO:, cols] / jnp.minimum(t1, float(w)) - cur[:, cols]
        mixed.append(_dot(pooled.astype(_BF16), pw_ref[g]))
    mixed = (jnp.concatenate(mixed, axis=-1) * ps_ref[...]).astype(_BF16)
    merged = (sga_ref[...].astype(_F32) * _dot(attn_ref[...], wba_ref[...])
              + sgb_ref[...].astype(_F32) * _dot(mixed, wbp_ref[...]))
    o_ref[...] = x_ref[...] + _dot(merged.astype(_BF16), wo_ref[...])


def _merge(x, attn, pool, sga, sgb, wba, pw, ps, wbp, wo, seq):
    n = x.shape[0]
    tps = seq // TOKEN_TILE
    halo_blocks = TOKEN_TILE // POOL_HALO

    def flat(width):
        return pl.BlockSpec((TOKEN_TILE, width), lambda i: (i, 0))

    return pl.pallas_call(
        functools.partial(_merge_kernel, tiles_per_seq=tps),
        grid=(n // TOKEN_TILE,),
        in_specs=[
            flat(D_MODEL), flat(N_HEADS * HEAD_DIM), flat(POOL_WIDTH),
            pl.BlockSpec((POOL_HALO, POOL_WIDTH), lambda i: (jnp.maximum(i * halo_blocks - 1, 0), 0)),
            flat(D_MODEL), flat(D_MODEL),
            _const_spec((N_HEADS * HEAD_DIM, D_MODEL)), _const_spec((len(POOL_WINDOWS), POOL_GROUP, POOL_GROUP)),
            _const_spec((1, POOL_WIDTH)), _const_spec((POOL_WIDTH, D_MODEL)), _const_spec((D_MODEL, D_MODEL)),
        ],
        out_specs=flat(D_MODEL),
        out_shape=jax.ShapeDtypeStruct((n, D_MODEL), _F32),
        compiler_params=pltpu.CompilerParams(dimension_semantics=("arbitrary",), vmem_limit_bytes=VMEM_LIMIT_BYTES),
        name="merge_out",
    )(x, attn, pool, pool, sga, sgb, wba, pw, ps, wbp, wo)


def _split_w_in(w):
    cuts, off = {}, 0
    for name, width in (("q", N_HEADS * HEAD_DIM), ("ckv", KV_LATENT), ("qi", IDX_HEADS * IDX_DIM), ("ki", IDX_DIM),
                        ("wi", IDX_HEADS), ("pool", POOL_WIDTH), ("ga", D_MODEL), ("gb", D_MODEL)):
        cuts[name] = w[:, off:off + width]
        off += width
    pad_n = jnp.zeros((w.shape[0], LANES - IDX_DIM), w.dtype)
    wn = jnp.concatenate([cuts["ckv"], cuts["pool"], cuts["ga"], cuts["gb"], cuts["ki"], pad_n], axis=1)
    pad_t = jnp.zeros((w.shape[0], BF16_SUBLANES - IDX_HEADS), w.dtype)
    wt = jnp.concatenate([cuts["q"], cuts["qi"], cuts["wi"], pad_t], axis=1).T
    return wn.astype(_BF16), wt.astype(_BF16)


def kernel(x, norm_ffn1, ffn1_gate, ffn1_up, ffn1_down, norm_mix, w_in, norm_kv, w_uk, w_uv, pool_w, pool_scale,
           w_branch_attn, w_branch_pool, w_out, norm_ffn2, ffn2_gate, ffn2_up, ffn2_down, norm_final):
    batch, seq, _ = x.shape
    depth = norm_ffn1.shape[0]
    assert seq % TOKEN_TILE == 0 and seq % Q_TILE == 0 and batch % SEQ_PER_STEP == 0
    assert seq < _HALF
    k_sel = min(TOPK_MAX, seq // 4)
    n = batch * seq
    bf = lambda a: a.astype(_BF16)
    row = lambda a: a.reshape(1, -1)
    gf = row(norm_final)
    h = x.reshape(n, D_MODEL)
    for i in range(depth):
        h = _ffn(h, row(norm_ffn1[i]), bf(ffn1_gate[i]), bf(ffn1_up[i]), bf(ffn1_down[i]), gf, False)
        wn, wt = _split_w_in(w_in[i])
        wuk_h = bf(jnp.transpose(w_uk[i], (1, 0, 2)))
        wuvt_h = bf(jnp.transpose(w_uv[i], (1, 2, 0)))
        qlt, ckv, ckvt, qit, ki, wit, pool, sga, sgb = _proj(h, row(norm_mix[i]), wn, wt, row(norm_kv[i]), wuk_h,
                                                             batch, seq)
        attn = _attn(qlt, qit, wit, ki.reshape(batch, seq, -1), ckv.reshape(batch, seq, -1), ckvt, wuvt_h, k_sel)
        h = _merge(h, attn.reshape(n, -1), pool, sga, sgb, bf(w_branch_attn[i]), bf(pool_w[i]), row(pool_scale[i]),
                   bf(w_branch_pool[i]), bf(w_out[i]), seq)
        h = _ffn(h, row(norm_ffn2[i]), bf(ffn2_gate[i]), bf(ffn2_up[i]), bf(ffn2_down[i]), gf, i == depth - 1)
    return h.reshape(batch, seq, D_MODEL)
```

```python
import functools

import jax
import jax.numpy as jnp
from jax import lax
from jax.experimental import pallas as pl
from jax.experimental.pallas import tpu as pltpu

D_MODEL = 1024
N_HEADS = 8
HEAD_DIM = 64
KV_LATENT = 128
ATTN_SCALE = HEAD_DIM ** -0.5
LOG2_E = 1.4426950408889634
IDX_HEADS = 8
IDX_DIM = 64
IDX_SCALE = (IDX_HEADS ** -0.5) * (IDX_DIM ** -0.5)
TOPK_MAX = 256
POOL_WINDOWS = (2, 4, 8, 16)
POOL_GROUP = 128
POOL_WIDTH = POOL_GROUP * len(POOL_WINDOWS)
POOL_HALO = 16
D_FF = 2816
EPS = 1e-6

LANES = 128
SUBLANES = 8
WORD_BITS = 32
BF16_SUBLANES = 16
KV_ROWS = KV_LATENT + BF16_SUBLANES
VMEM_LIMIT_BYTES = 56 * 1024 * 1024

TOKEN_TILE = 512
FF_CHUNK = 256
Q_TILE = 256
K_CHUNK = SUBLANES * WORD_BITS
assert K_CHUNK == Q_TILE
ATTN_ROWS = 2 * K_CHUNK
SEQ_PER_STEP = 4

_N_CKV = 0
_N_POOL = _N_CKV + KV_LATENT
_N_GA = _N_POOL + POOL_WIDTH
_N_GB = _N_GA + D_MODEL
_N_KI = _N_GB + D_MODEL
_N_TOTAL = _N_KI + LANES
_T_Q = 0
_T_QI = _T_Q + N_HEADS * HEAD_DIM
_T_WI = _T_QI + IDX_HEADS * IDX_DIM
_T_TOTAL = _T_WI + BF16_SUBLANES

_F32 = jnp.float32
_BF16 = jnp.bfloat16
_INT_MIN = -(2 ** 31)
_NEG_BIG = float(jnp.finfo(jnp.float32).min)


def _const_spec(shape):
    return pl.BlockSpec(shape, lambda *_: (0,) * len(shape), pipeline_mode=pl.Buffered(1))


def _rms(x, g):
    return x * lax.rsqrt(jnp.mean(x * x, axis=-1, keepdims=True) + EPS) * g


def _dot(a, b):
    return jnp.dot(a, b, preferred_element_type=_F32)


def _ffn_kernel(x_ref, g_ref, wg_ref, wu_ref, wd_ref, gf_ref, o_ref, *, final_norm):
    x = x_ref[...]
    h = _rms(x, g_ref[...]).astype(_BF16)
    acc = jnp.zeros(x.shape, _F32)
    for c in range(D_FF // FF_CHUNK):
        sl = slice(c * FF_CHUNK, (c + 1) * FF_CHUNK)
        gate = _dot(h, wg_ref[:, sl])
        up = _dot(h, wu_ref[:, sl])
        act = (gate * jax.nn.sigmoid(gate) * up).astype(_BF16)
        acc = acc + _dot(act, wd_ref[sl, :])
    y = x + 0.5 * acc
    if final_norm:
        y = _rms(y, gf_ref[...])
    o_ref[...] = y


def _ffn(x, g, wg, wu, wd, gf, final_norm):
    n = x.shape[0]
    tile = pl.BlockSpec((TOKEN_TILE, D_MODEL), lambda i: (i, 0))
    return pl.pallas_call(
        functools.partial(_ffn_kernel, final_norm=final_norm),
        grid=(n // TOKEN_TILE,),
        in_specs=[tile, _const_spec((1, D_MODEL)), _const_spec((D_MODEL, D_FF)), _const_spec((D_MODEL, D_FF)),
                  _const_spec((D_FF, D_MODEL)), _const_spec((1, D_MODEL))],
        out_specs=tile,
        out_shape=jax.ShapeDtypeStruct((n, D_MODEL), _F32),
        compiler_params=pltpu.CompilerParams(dimension_semantics=("arbitrary",), vmem_limit_bytes=VMEM_LIMIT_BYTES),
        name="ffn",
    )(x, g, wg, wu, wd, gf)


def _proj_kernel(x_ref, g_ref, wn_ref, wt_ref, gkv_ref, wuk_ref, qlt_ref, ckv_ref, ckvt_ref, qit_ref, ki_ref, wit_ref,
                 pool_ref, sga_ref, sgb_ref):
    h32 = _rms(x_ref[...], g_ref[...])
    h = h32.astype(_BF16)
    ht = h32.T.astype(_BF16)

    def z(lo, hi):
        return _dot(h, wn_ref[:, lo:hi])

    def zt(lo, hi):
        return _dot(wt_ref[lo:hi, :], ht)

    ckv = _rms(z(_N_CKV, _N_POOL), gkv_ref[...])
    ckv_ref[...] = ckv.astype(_BF16)
    ones_tile = (lax.broadcasted_iota(jnp.int32, (BF16_SUBLANES, TOKEN_TILE), 0) == 0).astype(_F32)
    ckvt_ref[0] = jnp.concatenate([ckv.T, ones_tile], axis=0).astype(_BF16)
    pool_ref[...] = z(_N_POOL, _N_GA)
    sga_ref[...] = jax.nn.sigmoid(z(_N_GA, _N_GB)).astype(_BF16)
    sgb_ref[...] = jax.nn.sigmoid(z(_N_GB, _N_KI)).astype(_BF16)
    ki_ref[...] = z(_N_KI, _N_TOTAL)[:, :IDX_DIM].astype(_BF16)

    qt = zt(_T_Q, _T_QI).astype(_BF16)
    for hd in range(N_HEADS):
        qlt = (_dot(wuk_ref[hd], qt[hd * HEAD_DIM:(hd + 1) * HEAD_DIM, :]) * (ATTN_SCALE * LOG2_E)).astype(_BF16)
        for j in range(TOKEN_TILE // Q_TILE):
            qlt_ref[0, j, :, hd * Q_TILE:(hd + 1) * Q_TILE] = qlt[:, j * Q_TILE:(j + 1) * Q_TILE]
    qit_ref[0] = zt(_T_QI, _T_WI).astype(_BF16)
    wit_ref[0] = zt(_T_WI, _T_TOTAL)[:IDX_HEADS, :] * IDX_SCALE


def _proj(x, g, wn, wt, gkv, wuk, batch, seq):
    n = x.shape[0]
    tps = seq // TOKEN_TILE

    def flat(width):
        return pl.BlockSpec((TOKEN_TILE, width), lambda i: (i, 0))

    def feat(rows):
        return pl.BlockSpec((1, rows, TOKEN_TILE), lambda i: (i // tps, 0, i % tps))

    out_shape = [
        jax.ShapeDtypeStruct((batch, seq // Q_TILE, KV_LATENT, N_HEADS * Q_TILE), _BF16),
        jax.ShapeDtypeStruct((n, KV_LATENT), _BF16),
        jax.ShapeDtypeStruct((batch, KV_ROWS, seq), _BF16),
        jax.ShapeDtypeStruct((batch, IDX_HEADS * IDX_DIM, seq), _BF16),
        jax.ShapeDtypeStruct((n, IDX_DIM), _BF16),
        jax.ShapeDtypeStruct((batch, IDX_HEADS, seq), _F32),
        jax.ShapeDtypeStruct((n, POOL_WIDTH), _F32),
        jax.ShapeDtypeStruct((n, D_MODEL), _BF16),
        jax.ShapeDtypeStruct((n, D_MODEL), _BF16),
    ]
    out_specs = [
        pl.BlockSpec((1, TOKEN_TILE // Q_TILE, KV_LATENT, N_HEADS * Q_TILE), lambda i: (i // tps, i % tps, 0, 0)),
        flat(KV_LATENT), feat(KV_ROWS), feat(IDX_HEADS * IDX_DIM), flat(IDX_DIM), feat(IDX_HEADS),
        flat(POOL_WIDTH), flat(D_MODEL), flat(D_MODEL),
    ]
    return pl.pallas_call(
        _proj_kernel,
        grid=(n // TOKEN_TILE,),
        in_specs=[flat(D_MODEL), _const_spec((1, D_MODEL)), _const_spec((D_MODEL, _N_TOTAL)),
                  _const_spec((_T_TOTAL, D_MODEL)), _const_spec((1, KV_LATENT)),
                  _const_spec((N_HEADS, KV_LATENT, HEAD_DIM))],
        out_specs=out_specs,
        out_shape=out_shape,
        compiler_params=pltpu.CompilerParams(dimension_semantics=("arbitrary",), vmem_limit_bytes=VMEM_LIMIT_BYTES),
        name="mix_proj",
    )(x, g, wn, wt, gkv, wuk)


def _bit_planes(u):
    a = [u[j * SUBLANES:(j + 1) * SUBLANES, :] for j in range(WORD_BITS)]
    j, m = WORD_BITS // 2, 0x0000FFFF
    while j:
        for k in range(WORD_BITS):
            if k & j == 0:
                t = (a[k] ^ (a[k + j] >> j)) & m
                a[k] = a[k] ^ t
                a[k + j] = a[k + j] ^ (t << j)
        j >>= 1
        m = m ^ (m << j)
    return a


def _popcount_rows(words):
    return jnp.sum(lax.population_count(words), axis=0, keepdims=True)


def _attn_kernel(qlt_ref, qit_ref, wit_ref, ki_ref, ckv_ref, ckvt_ref, wuvt_ref, o_ref, keys_ref, planes_ref, eq_ref,
                 thr_ref, ngt_ref, neq_ref, m_ref, acc_ref, *, k_sel):
    qb = pl.program_id(1)
    n_chunks = qb + 1
    q_pos = qb * Q_TILE + lax.broadcasted_iota(jnp.int32, (1, Q_TILE), 1)
    k_eff = jnp.minimum(k_sel, q_pos + 1)
    row_iota = lax.broadcasted_iota(jnp.int32, (K_CHUNK, Q_TILE), 0)
    seqs = range(SEQ_PER_STEP)

    def chunk_start(c):
        return pl.multiple_of(c * K_CHUNK, K_CHUNK)

    def score_chunk(c, diagonal):
        k0 = chunk_start(c)
        for s in seqs:
            kc = ki_ref[s, pl.ds(k0, K_CHUNK), :]
            score = jnp.zeros((K_CHUNK, Q_TILE), _F32)
            for j in range(IDX_HEADS):
                r = _dot(kc, qit_ref[s, j * IDX_DIM:(j + 1) * IDX_DIM, :])
                score = score + jnp.maximum(r, 0.0) * wit_ref[s, j:j + 1, :]
            bits = pltpu.bitcast(score, jnp.int32)
            key = bits ^ ((bits >> 31) & 0x7FFFFFFF)
            if diagonal:
                key = jnp.where(k0 + row_iota <= q_pos, key, _INT_MIN)
            keys_ref[s, pl.ds(k0, K_CHUNK), :] = key
            for p, plane in enumerate(_bit_planes(key ^ _INT_MIN)):
                planes_ref[s, c, p] = plane

    def score_full_chunk(c, _):
        score_chunk(c, False)
        return 0

    lax.fori_loop(0, qb, score_full_chunk, 0)
    score_chunk(qb, True)

    def search(nc):
        every = jnp.full((SUBLANES, Q_TILE), -1, jnp.int32)
        none = jnp.zeros((1, Q_TILE), jnp.int32)
        width = nc + 2

        def decide_bit(p, carry):
            out = []
            for s in seqs:
                alive, n_gt, thr_u = carry[s * width:s * width + nc], carry[s * width + nc], carry[s * width + nc + 1]
                planes = [planes_ref[s, c, pl.ds(p, 1)][0] for c in range(nc)]
                ones = functools.reduce(jnp.add, [lax.population_count(a & w) for a, w in zip(alive, planes)])
                n_ones = jnp.sum(ones, axis=0, keepdims=True)
                take = n_gt + n_ones >= k_eff
                thr_u = thr_u | jnp.where(take, jnp.left_shift(jnp.int32(1), WORD_BITS - 1 - p), 0)
                n_gt = jnp.where(take, n_gt, n_gt + n_ones)
                flip = jnp.where(take, 0, -1)
                out += [a & (w ^ flip) for a, w in zip(alive, planes)] + [n_gt, thr_u]
            return tuple(out)

        res = lax.fori_loop(0, WORD_BITS, decide_bit, tuple(([every] * nc + [none, none]) * SEQ_PER_STEP))
        for s in seqs:
            alive = res[s * width:s * width + nc]
            thr_ref[s] = res[s * width + nc + 1] ^ _INT_MIN
            ngt_ref[s] = res[s * width + nc]
            neq_ref[s] = functools.reduce(jnp.add, [_popcount_rows(a) for a in alive])
            for c in range(nc):
                eq_ref[s, c] = alive[c]

    n_max = keys_ref.shape[1] // K_CHUNK
    for nc in range(1, n_max + 1):
        @pl.when(n_chunks == nc)
        def _(nc=nc):
            if nc * K_CHUNK <= k_sel:
                for s in seqs:
                    thr_ref[s] = jnp.full((1, Q_TILE), _INT_MIN + 1, jnp.int32)
                    ngt_ref[s] = k_eff
                    neq_ref[s] = jnp.zeros((1, Q_TILE), jnp.int32)
            else:
                search(nc)

    thr = [thr_ref[s] for s in seqs]
    need = [k_eff - ngt_ref[s] for s in seqs]
    surplus = [neq_ref[s] - need[s] for s in seqs]

    @pl.when(functools.reduce(jnp.maximum, [jnp.max(x) for x in surplus]) > 0)
    def _():
        sublane = lax.broadcasted_iota(jnp.int32, (SUBLANES, Q_TILE), 0)
        group_bits = WORD_BITS.bit_length() - 1
        row_bits = SUBLANES.bit_length() - 1

        def ties_before(limits):
            within = []
            for lim in limits:
                group = (lim >> row_bits) & (WORD_BITS - 1)
                this_bit = jnp.left_shift(jnp.int32(1), WORD_BITS - 1 - group)
                earlier_groups = ~(jnp.left_shift(jnp.int32(2), WORD_BITS - 1 - group) - 1)
                within.append(earlier_groups | jnp.where(sublane < (lim & (SUBLANES - 1)), this_bit, 0))

            def body(c, accs):
                out = []
                for s in seqs:
                    c_lim = limits[s] >> (row_bits + group_bits)
                    mask = jnp.where(c < c_lim, -1, jnp.where(c == c_lim, within[s], 0))
                    out.append(accs[s] + lax.population_count(eq_ref[s, c] & mask))
                return tuple(out)

            zero = jnp.zeros((SUBLANES, Q_TILE), jnp.int32)
            accs = lax.fori_loop(0, n_chunks, body, (zero,) * SEQ_PER_STEP)
            return [jnp.sum(a, axis=0, keepdims=True) for a in accs]

        pos_bits = keys_ref.shape[1].bit_length()

        def grow(it, limits):
            cands = [lim + jnp.left_shift(jnp.int32(1), pos_bits - 1 - it) for lim in limits]
            n_before = ties_before(cands)
            return tuple(jnp.where(n_before[s] <= need[s], cands[s], limits[s]) for s in seqs)

        limits = lax.fori_loop(0, pos_bits, grow, (jnp.zeros((1, Q_TILE), jnp.int32),) * SEQ_PER_STEP)

        def demote(c, _):
            k0 = chunk_start(c)
            for s in seqs:
                k = keys_ref[s, pl.ds(k0, K_CHUNK), :]
                drop = (k == thr[s]) & (k0 + row_iota >= limits[s])
                keys_ref[s, pl.ds(k0, K_CHUNK), :] = jnp.where(drop, thr[s] - 1, k)
            return 0

        lax.fori_loop(0, n_chunks, demote, 0)

    m_ref[...] = jnp.full(m_ref.shape, _NEG_BIG, _F32)
    acc_ref[...] = jnp.zeros(acc_ref.shape, _F32)

    def attend_rows(k0, size):
        for s in seqs:
            bias = jnp.where(keys_ref[s, pl.ds(k0, size), :] >= thr[s], 0.0, _NEG_BIG)
            ckv = ckv_ref[s, pl.ds(k0, size), :]
            ckvt = ckvt_ref[s, :, pl.ds(k0, size)]
            logit = _dot(ckv, qlt_ref[s, 0]) + jnp.concatenate([bias] * N_HEADS, axis=1)
            m_old = m_ref[s]
            m_new = jnp.maximum(m_old, jnp.max(logit, axis=0, keepdims=True))
            p = jnp.exp2(logit - m_new).astype(_BF16)
            acc_ref[s] = jnp.exp2(m_old - m_new) * acc_ref[s] + _dot(ckvt, p)
            m_ref[s] = m_new

    def attend_wide(c, _):
        attend_rows(pl.multiple_of(c * ATTN_ROWS, ATTN_ROWS), ATTN_ROWS)
        return 0

    per_wide = ATTN_ROWS // K_CHUNK
    lax.fori_loop(0, n_chunks // per_wide, attend_wide, 0)
    for r in range(1, per_wide):
        @pl.when(n_chunks % per_wide >= r)
        def _(r=r):
            attend_rows(chunk_start(n_chunks - n_chunks % per_wide + r - 1), K_CHUNK)

    for s in seqs:
        o_lat_t = (acc_ref[s, :KV_LATENT, :] / acc_ref[s, KV_LATENT:KV_LATENT + 1, :]).astype(_BF16)
        out_t = [_dot(wuvt_ref[hd], o_lat_t[:, hd * Q_TILE:(hd + 1) * Q_TILE]) for hd in range(N_HEADS)]
        o_ref[s] = jnp.concatenate(out_t, axis=0).T.astype(_BF16)


def _attn(qlt, qit, wit, ki, ckv, ckvt, wuvt, k_sel):
    batch, _, seq = ckvt.shape
    sp = SEQ_PER_STEP
    return pl.pallas_call(
        functools.partial(_attn_kernel, k_sel=k_sel),
        grid=(batch // sp, seq // Q_TILE),
        in_specs=[
            pl.BlockSpec((sp, 1, KV_LATENT, N_HEADS * Q_TILE), lambda b, i: (b, i, 0, 0)),
            pl.BlockSpec((sp, IDX_HEADS * IDX_DIM, Q_TILE), lambda b, i: (b, 0, i)),
            pl.BlockSpec((sp, IDX_HEADS, Q_TILE), lambda b, i: (b, 0, i)),
            pl.BlockSpec((sp, seq, IDX_DIM), lambda b, i: (b, 0, 0)),
            pl.BlockSpec((sp, seq, KV_LATENT), lambda b, i: (b, 0, 0)),
            pl.BlockSpec((sp, KV_ROWS, seq), lambda b, i: (b, 0, 0)),
            _const_spec((N_HEADS, HEAD_DIM, KV_LATENT)),
        ],
        out_specs=pl.BlockSpec((sp, Q_TILE, N_HEADS * HEAD_DIM), lambda b, i: (b, i, 0)),
        out_shape=jax.ShapeDtypeStruct((batch, seq, N_HEADS * HEAD_DIM), _BF16),
        scratch_shapes=[pltpu.VMEM((sp, seq, Q_TILE), jnp.int32),
                        pltpu.VMEM((sp, seq // K_CHUNK, WORD_BITS, SUBLANES, Q_TILE), jnp.int32),
                        pltpu.VMEM((sp, seq // K_CHUNK, SUBLANES, Q_TILE), jnp.int32),
                        pltpu.VMEM((sp, 1, Q_TILE), jnp.int32), pltpu.VMEM((sp, 1, Q_TILE), jnp.int32),
                        pltpu.VMEM((sp, 1, Q_TILE), jnp.int32), pltpu.VMEM((sp, 1, N_HEADS * Q_TILE), _F32),
                        pltpu.VMEM((sp, KV_ROWS, N_HEADS * Q_TILE), _F32)],
        compiler_params=pltpu.CompilerParams(dimension_semantics=("arbitrary", "arbitrary"),
                                             vmem_limit_bytes=VMEM_LIMIT_BYTES),
        name="dsa_attn",
    )(qlt, qit, wit, ki, ckv, ckvt, wuvt)


def _merge_kernel(x_ref, attn_ref, pool_ref, halo_ref, sga_ref, sgb_ref, wba_ref, pw_ref, ps_ref, wbp_ref, wo_ref,
                  o_ref, *, tiles_per_seq):
    tile = pl.program_id(0) % tiles_per_seq
    cur = pool_ref[...]
    halo = jnp.where(tile == 0, 0.0, halo_ref[...])
    ext = jnp.concatenate([halo, cur], axis=0)
    t1 = (tile * TOKEN_TILE + 1 + lax.broadcasted_iota(jnp.int32, (TOKEN_TILE, 1), 0)).astype(_F32)
    mixed = []
    win_sum, width = ext, 1
    for g, w in enumerate(POOL_WINDOWS):
        while width < w:
            win_sum = win_sum + pltpu.roll(win_sum, width, 0)
            width *= 2
        cols = slice(g * POOL_GROUP, (g + 1) * POOL_GROUP)
        pooled = win_sum[POOL_HALO:, cols] / jnp.minimum(t1, float(w)) - cur[:, cols]
        mixed.append(_dot(pooled.astype(_BF16), pw_ref[g]))
    mixed = (jnp.concatenate(mixed, axis=-1) * ps_ref[...]).astype(_BF16)
    merged = (sga_ref[...].astype(_F32) * _dot(attn_ref[...], wba_ref[...])
              + sgb_ref[...].astype(_F32) * _dot(mixed, wbp_ref[...]))
    o_ref[...] = x_ref[...] + _dot(merged.astype(_BF16), wo_ref[...])


def _merge(x, attn, pool, sga, sgb, wba, pw, ps, wbp, wo, seq):
    n = x.shape[0]
    tps = seq // TOKEN_TILE
    halo_blocks = TOKEN_TILE // POOL_HALO

    def flat(width):
        return pl.BlockSpec((TOKEN_TILE, width), lambda i: (i, 0))

    return pl.pallas_call(
        functools.partial(_merge_kernel, tiles_per_seq=tps),
        grid=(n // TOKEN_TILE,),
        in_specs=[
            flat(D_MODEL), flat(N_HEADS * HEAD_DIM), flat(POOL_WIDTH),
            pl.BlockSpec((POOL_HALO, POOL_WIDTH), lambda i: (jnp.maximum(i * halo_blocks - 1, 0), 0)),
            flat(D_MODEL), flat(D_MODEL),
            _const_spec((N_HEADS * HEAD_DIM, D_MODEL)), _const_spec((len(POOL_WINDOWS), POOL_GROUP, POOL_GROUP)),
            _const_spec((1, POOL_WIDTH)), _const_spec((POOL_WIDTH, D_MODEL)), _const_spec((D_MODEL, D_MODEL)),
        ],
        out_specs=flat(D_MODEL),
        out_shape=jax.ShapeDtypeStruct((n, D_MODEL), _F32),
        compiler_params=pltpu.CompilerParams(dimension_semantics=("arbitrary",), vmem_limit_bytes=VMEM_LIMIT_BYTES),
        name="merge_out",
    )(x, attn, pool, pool, sga, sgb, wba, pw, ps, wbp, wo)


def _split_w_in(w):
    cuts, off = {}, 0
    for name, width in (("q", N_HEADS * HEAD_DIM), ("ckv", KV_LATENT), ("qi", IDX_HEADS * IDX_DIM), ("ki", IDX_DIM),
                        ("wi", IDX_HEADS), ("pool", POOL_WIDTH), ("ga", D_MODEL), ("gb", D_MODEL)):
        cuts[name] = w[:, off:off + width]
        off += width
    pad_n = jnp.zeros((w.shape[0], LANES - IDX_DIM), w.dtype)
    wn = jnp.concatenate([cuts["ckv"], cuts["pool"], cuts["ga"], cuts["gb"], cuts["ki"], pad_n], axis=1)
    pad_t = jnp.zeros((w.shape[0], BF16_SUBLANES - IDX_HEADS), w.dtype)
    wt = jnp.concatenate([cuts["q"], cuts["qi"], cuts["wi"], pad_t], axis=1).T
    return wn.astype(_BF16), wt.astype(_BF16)


def kernel(x, norm_ffn1, ffn1_gate, ffn1_up, ffn1_down, norm_mix, w_in, norm_kv, w_uk, w_uv, pool_w, pool_scale,
           w_branch_attn, w_branch_pool, w_out, norm_ffn2, ffn2_gate, ffn2_up, ffn2_down, norm_final):
    batch, seq, _ = x.shape
    depth = norm_ffn1.shape[0]
    assert seq % TOKEN_TILE == 0 and seq % Q_TILE == 0 and batch % SEQ_PER_STEP == 0
    k_sel = min(TOPK_MAX, seq // 4)
    n = batch * seq
    bf = lambda a: a.astype(_BF16)
    row = lambda a: a.reshape(1, -1)
    gf = row(norm_final)
    h = x.reshape(n, D_MODEL)
    for i in range(depth):
        h = _ffn(h, row(norm_ffn1[i]), bf(ffn1_gate[i]), bf(ffn1_up[i]), bf(ffn1_down[i]), gf, False)
        wn, wt = _split_w_in(w_in[i])
        wuk_h = bf(jnp.transpose(w_uk[i], (1, 0, 2)))
        wuvt_h = bf(jnp.transpose(w_uv[i], (1, 2, 0)))
        qlt, ckv, ckvt, qit, ki, wit, pool, sga, sgb = _proj(h, row(norm_mix[i]), wn, wt, row(norm_kv[i]), wuk_h,
                                                             batch, seq)
        attn = _attn(qlt, qit, wit, ki.reshape(batch, seq, -1), ckv.reshape(batch, seq, -1), ckvt, wuvt_h, k_sel)
        h = _merge(h, attn.reshape(n, -1), pool, sga, sgb, bf(w_branch_attn[i]), bf(pool_w[i]), row(pool_scale[i]),
                   bf(w_branch_pool[i]), bf(w_out[i]), seq)
        h = _ffn(h, row(norm_ffn2[i]), bf(ffn2_gate[i]), bf(ffn2_up[i]), bf(ffn2_down[i]), gf, i == depth - 1)
    return h.reshape(batch, seq, D_MODEL)
```

```python
import functools

import jax
import jax.numpy as jnp
from jax import lax
from jax.experimental import pallas as pl
from jax.experimental.pallas import tpu as pltpu

D_MODEL = 1024
N_HEADS = 8
HEAD_DIM = 64
KV_LATENT = 128
ATTN_SCALE = HEAD_DIM ** -0.5
LOG2_E = 1.4426950408889634
IDX_HEADS = 8
IDX_DIM = 64
IDX_SCALE = (IDX_HEADS ** -0.5) * (IDX_DIM ** -0.5)
TOPK_MAX = 256
POOL_WINDOWS = (2, 4, 8, 16)
POOL_GROUP = 128
POOL_WIDTH = POOL_GROUP * len(POOL_WINDOWS)
POOL_HALO = 16
D_FF = 2816
EPS = 1e-6

LANES = 128
SUBLANES = 8
WORD_BITS = 32
BF16_SUBLANES = 16
KV_ROWS = KV_LATENT + BF16_SUBLANES
VMEM_LIMIT_BYTES = 56 * 1024 * 1024

TOKEN_TILE = 1024
FF_CHUNK = 256
Q_TILE = 256
K_CHUNK = SUBLANES * WORD_BITS
assert K_CHUNK == Q_TILE
ATTN_ROWS = 2 * K_CHUNK
SEQ_PER_STEP = 4

_N_CKV = 0
_N_POOL = _N_CKV + KV_LATENT
_N_GA = _N_POOL + POOL_WIDTH
_N_GB = _N_GA + D_MODEL
_N_KI = _N_GB + D_MODEL
_N_TOTAL = _N_KI + LANES
_T_Q = 0
_T_QI = _T_Q + N_HEADS * HEAD_DIM
_T_WI = _T_QI + IDX_HEADS * IDX_DIM
_T_TOTAL = _T_WI + BF16_SUBLANES

_F32 = jnp.float32
_BF16 = jnp.bfloat16
_INT_MIN = -(2 ** 31)
_NEG_BIG = float(jnp.finfo(jnp.float32).min)


def _const_spec(shape):
    return pl.BlockSpec(shape, lambda *_: (0,) * len(shape), pipeline_mode=pl.Buffered(1))


def _rms(x, g):
    return x * lax.rsqrt(jnp.mean(x * x, axis=-1, keepdims=True) + EPS) * g


def _dot(a, b):
    return jnp.dot(a, b, preferred_element_type=_F32)


def _ffn_kernel(x_ref, g_ref, wg_ref, wu_ref, wd_ref, gf_ref, o_ref, *, final_norm):
    x = x_ref[...]
    h = _rms(x, g_ref[...]).astype(_BF16)
    acc = jnp.zeros(x.shape, _F32)
    for c in range(D_FF // FF_CHUNK):
        sl = slice(c * FF_CHUNK, (c + 1) * FF_CHUNK)
        gate = _dot(h, wg_ref[:, sl])
        up = _dot(h, wu_ref[:, sl])
        act = (gate * jax.nn.sigmoid(gate) * up).astype(_BF16)
        acc = acc + _dot(act, wd_ref[sl, :])
    y = x + 0.5 * acc
    if final_norm:
        y = _rms(y, gf_ref[...])
    o_ref[...] = y


def _ffn(x, g, wg, wu, wd, gf, final_norm):
    n = x.shape[0]
    tile = pl.BlockSpec((TOKEN_TILE, D_MODEL), lambda i: (i, 0))
    return pl.pallas_call(
        functools.partial(_ffn_kernel, final_norm=final_norm),
        grid=(n // TOKEN_TILE,),
        in_specs=[tile, _const_spec((1, D_MODEL)), _const_spec((D_MODEL, D_FF)), _const_spec((D_MODEL, D_FF)),
                  _const_spec((D_FF, D_MODEL)), _const_spec((1, D_MODEL))],
        out_specs=tile,
        out_shape=jax.ShapeDtypeStruct((n, D_MODEL), _F32),
        compiler_params=pltpu.CompilerParams(dimension_semantics=("arbitrary",), vmem_limit_bytes=VMEM_LIMIT_BYTES),
        name="ffn",
    )(x, g, wg, wu, wd, gf)


def _proj_kernel(x_ref, g_ref, wn_ref, wt_ref, gkv_ref, wuk_ref, qlt_ref, ckv_ref, ckvt_ref, qit_ref, ki_ref, wit_ref,
                 pool_ref, sga_ref, sgb_ref):
    h32 = _rms(x_ref[...], g_ref[...])
    h = h32.astype(_BF16)
    ht = h32.T.astype(_BF16)

    def z(lo, hi):
        return _dot(h, wn_ref[:, lo:hi])

    def zt(lo, hi):
        return _dot(wt_ref[lo:hi, :], ht)

    ckv = _rms(z(_N_CKV, _N_POOL), gkv_ref[...])
    ckv_ref[...] = ckv.astype(_BF16)
    ones_tile = (lax.broadcasted_iota(jnp.int32, (BF16_SUBLANES, TOKEN_TILE), 0) == 0).astype(_F32)
    ckvt_ref[0] = jnp.concatenate([ckv.T, ones_tile], axis=0).astype(_BF16)
    pool_ref[...] = z(_N_POOL, _N_GA)
    sga_ref[...] = jax.nn.sigmoid(z(_N_GA, _N_GB)).astype(_BF16)
    sgb_ref[...] = jax.nn.sigmoid(z(_N_GB, _N_KI)).astype(_BF16)
    ki_ref[...] = z(_N_KI, _N_TOTAL)[:, :IDX_DIM].astype(_BF16)

    qt = zt(_T_Q, _T_QI).astype(_BF16)
    for hd in range(N_HEADS):
        qlt = (_dot(wuk_ref[hd], qt[hd * HEAD_DIM:(hd + 1) * HEAD_DIM, :]) * (ATTN_SCALE * LOG2_E)).astype(_BF16)
        for j in range(TOKEN_TILE // Q_TILE):
            qlt_ref[0, j, :, hd * Q_TILE:(hd + 1) * Q_TILE] = qlt[:, j * Q_TILE:(j + 1) * Q_TILE]
    qit_ref[0] = zt(_T_QI, _T_WI).astype(_BF16)
    wit_ref[0] = zt(_T_WI, _T_TOTAL)[:IDX_HEADS, :] * IDX_SCALE


def _proj(x, g, wn, wt, gkv, wuk, batch, seq):
    n = x.shape[0]
    tps = seq // TOKEN_TILE

    def flat(width):
        return pl.BlockSpec((TOKEN_TILE, width), lambda i: (i, 0))

    def feat(rows):
        return pl.BlockSpec((1, rows, TOKEN_TILE), lambda i: (i // tps, 0, i % tps))

    out_shape = [
        jax.ShapeDtypeStruct((batch, seq // Q_TILE, KV_LATENT, N_HEADS * Q_TILE), _BF16),
        jax.ShapeDtypeStruct((n, KV_LATENT), _BF16),
        jax.ShapeDtypeStruct((batch, KV_ROWS, seq), _BF16),
        jax.ShapeDtypeStruct((batch, IDX_HEADS * IDX_DIM, seq), _BF16),
        jax.ShapeDtypeStruct((n, IDX_DIM), _BF16),
        jax.ShapeDtypeStruct((batch, IDX_HEADS, seq), _F32),
        jax.ShapeDtypeStruct((n, POOL_WIDTH), _F32),
        jax.ShapeDtypeStruct((n, D_MODEL), _BF16),
        jax.ShapeDtypeStruct((n, D_MODEL), _BF16),
    ]
    out_specs = [
        pl.BlockSpec((1, TOKEN_TILE // Q_TILE, KV_LATENT, N_HEADS * Q_TILE), lambda i: (i // tps, i % tps, 0, 0)),
        flat(KV_LATENT), feat(KV_ROWS), feat(IDX_HEADS * IDX_DIM), flat(IDX_DIM), feat(IDX_HEADS),
        flat(POOL_WIDTH), flat(D_MODEL), flat(D_MODEL),
    ]
    return pl.pallas_call(
        _proj_kernel,
        grid=(n // TOKEN_TILE,),
        in_specs=[flat(D_MODEL), _const_spec((1, D_MODEL)), _const_spec((D_MODEL, _N_TOTAL)),
                  _const_spec((_T_TOTAL, D_MODEL)), _const_spec((1, KV_LATENT)),
                  _const_spec((N_HEADS, KV_LATENT, HEAD_DIM))],
        out_specs=out_specs,
        out_shape=out_shape,
        compiler_params=pltpu.CompilerParams(dimension_semantics=("arbitrary",), vmem_limit_bytes=VMEM_LIMIT_BYTES),
        name="mix_proj",
    )(x, g, wn, wt, gkv, wuk)


def _bit_planes(u):
    a = [u[j * SUBLANES:(j + 1) * SUBLANES, :] for j in range(WORD_BITS)]
    j, m = WORD_BITS // 2, 0x0000FFFF
    while j:
        for k in range(WORD_BITS):
            if k & j == 0:
                t = (a[k] ^ (a[k + j] >> j)) & m
                a[k] = a[k] ^ t
                a[k + j] = a[k + j] ^ (t << j)
        j >>= 1
        m = m ^ (m << j)
    return a


def _popcount_rows(words):
    return jnp.sum(lax.population_count(words), axis=0, keepdims=True)


def _attn_kernel(qlt_ref, qit_ref, wit_ref, ki_ref, ckv_ref, ckvt_ref, wuvt_ref, o_ref, keys_ref, planes_ref, eq_ref,
                 thr_ref, ngt_ref, neq_ref, m_ref, acc_ref, *, k_sel):
    qb = pl.program_id(1)
    n_chunks = qb + 1
    q_pos = qb * Q_TILE + lax.broadcasted_iota(jnp.int32, (1, Q_TILE), 1)
    k_eff = jnp.minimum(k_sel, q_pos + 1)
    row_iota = lax.broadcasted_iota(jnp.int32, (K_CHUNK, Q_TILE), 0)
    seqs = range(SEQ_PER_STEP)

    def chunk_start(c):
        return pl.multiple_of(c * K_CHUNK, K_CHUNK)

    def score_chunk(c, diagonal):
        k0 = chunk_start(c)
        for s in seqs:
            kc = ki_ref[s, pl.ds(k0, K_CHUNK), :]
            score = jnp.zeros((K_CHUNK, Q_TILE), _F32)
            for j in range(IDX_HEADS):
                r = _dot(kc, qit_ref[s, j * IDX_DIM:(j + 1) * IDX_DIM, :])
                score = score + jnp.maximum(r, 0.0) * wit_ref[s, j:j + 1, :]
            bits = pltpu.bitcast(score, jnp.int32)
            key = bits ^ ((bits >> 31) & 0x7FFFFFFF)
            if diagonal:
                key = jnp.where(k0 + row_iota <= q_pos, key, _INT_MIN)
            keys_ref[s, pl.ds(k0, K_CHUNK), :] = key
            for p, plane in enumerate(_bit_planes(key ^ _INT_MIN)):
                planes_ref[s, c, p] = plane

    def score_full_chunk(c, _):
        score_chunk(c, False)
        return 0

    lax.fori_loop(0, qb, score_full_chunk, 0)
    score_chunk(qb, True)

    def search(nc):
        every = jnp.full((SUBLANES, Q_TILE), -1, jnp.int32)
        none = jnp.zeros((1, Q_TILE), jnp.int32)
        width = nc + 2

        def decide_bit(p, carry):
            out = []
            for s in seqs:
                alive, n_gt, thr_u = carry[s * width:s * width + nc], carry[s * width + nc], carry[s * width + nc + 1]
                planes = [planes_ref[s, c, pl.ds(p, 1)][0] for c in range(nc)]
                ones = functools.reduce(jnp.add, [lax.population_count(a & w) for a, w in zip(alive, planes)])
                n_ones = jnp.sum(ones, axis=0, keepdims=True)
                take = n_gt + n_ones >= k_eff
                thr_u = thr_u | jnp.where(take, jnp.left_shift(jnp.int32(1), WORD_BITS - 1 - p), 0)
                n_gt = jnp.where(take, n_gt, n_gt + n_ones)
                flip = jnp.where(take, 0, -1)
                out += [a & (w ^ flip) for a, w in zip(alive, planes)] + [n_gt, thr_u]
            return tuple(out)

        res = lax.fori_loop(0, WORD_BITS, decide_bit, tuple(([every] * nc + [none, none]) * SEQ_PER_STEP))
        for s in seqs:
            alive = res[s * width:s * width + nc]
            thr_ref[s] = res[s * width + nc + 1] ^ _INT_MIN
            ngt_ref[s] = res[s * width + nc]
            neq_ref[s] = functools.reduce(jnp.add, [_popcount_rows(a) for a in alive])
            for c in range(nc):
                eq_ref[s, c] = alive[c]

    n_max = keys_ref.shape[1] // K_CHUNK
    for nc in range(1, n_max + 1):
        @pl.when(n_chunks == nc)
        def _(nc=nc):
            if nc * K_CHUNK <= k_sel:
                for s in seqs:
                    thr_ref[s] = jnp.full((1, Q_TILE), _INT_MIN + 1, jnp.int32)
                    ngt_ref[s] = k_eff
                    neq_ref[s] = jnp.zeros((1, Q_TILE), jnp.int32)
            else:
                search(nc)

    thr = [thr_ref[s] for s in seqs]
    need = [k_eff - ngt_ref[s] for s in seqs]
    surplus = [neq_ref[s] - need[s] for s in seqs]

    @pl.when(functools.reduce(jnp.maximum, [jnp.max(x) for x in surplus]) > 0)
    def _():
        sublane = lax.broadcasted_iota(jnp.int32, (SUBLANES, Q_TILE), 0)
        group_bits = WORD_BITS.bit_length() - 1
        row_bits = SUBLANES.bit_length() - 1

        def ties_before(limits):
            within = []
            for lim in limits:
                group = (lim >> row_bits) & (WORD_BITS - 1)
                this_bit = jnp.left_shift(jnp.int32(1), WORD_BITS - 1 - group)
                earlier_groups = ~(jnp.left_shift(jnp.int32(2), WORD_BITS - 1 - group) - 1)
                within.append(earlier_groups | jnp.where(sublane < (lim & (SUBLANES - 1)), this_bit, 0))

            def body(c, accs):
                out = []
                for s in seqs:
                    c_lim = limits[s] >> (row_bits + group_bits)
                    mask = jnp.where(c < c_lim, -1, jnp.where(c == c_lim, within[s], 0))
                    out.append(accs[s] + lax.population_count(eq_ref[s, c] & mask))
                return tuple(out)

            zero = jnp.zeros((SUBLANES, Q_TILE), jnp.int32)
            accs = lax.fori_loop(0, n_chunks, body, (zero,) * SEQ_PER_STEP)
            return [jnp.sum(a, axis=0, keepdims=True) for a in accs]

        pos_bits = keys_ref.shape[1].bit_length()

        def grow(it, limits):
            cands = [lim + jnp.left_shift(jnp.int32(1), pos_bits - 1 - it) for lim in limits]
            n_before = ties_before(cands)
            return tuple(jnp.where(n_before[s] <= need[s], cands[s], limits[s]) for s in seqs)

        limits = lax.fori_loop(0, pos_bits, grow, (jnp.zeros((1, Q_TILE), jnp.int32),) * SEQ_PER_STEP)

        def demote(c, _):
            k0 = chunk_start(c)
            for s in seqs:
                k = keys_ref[s, pl.ds(k0, K_CHUNK), :]
                drop = (k == thr[s]) & (k0 + row_iota >= limits[s])
                keys_ref[s, pl.ds(k0, K_CHUNK), :] = jnp.where(drop, thr[s] - 1, k)
            return 0

        lax.fori_loop(0, n_chunks, demote, 0)

    m_ref[...] = jnp.full(m_ref.shape, _NEG_BIG, _F32)
    acc_ref[...] = jnp.zeros(acc_ref.shape, _F32)

    def attend_rows(k0, size):
        for s in seqs:
            bias = jnp.where(keys_ref[s, pl.ds(k0, size), :] >= thr[s], 0.0, _NEG_BIG)
            ckv = ckv_ref[s, pl.ds(k0, size), :]
            ckvt = ckvt_ref[s, :, pl.ds(k0, size)]
            logit = _dot(ckv, qlt_ref[s, 0]) + jnp.concatenate([bias] * N_HEADS, axis=1)
            m_old = m_ref[s]
            m_new = jnp.maximum(m_old, jnp.max(logit, axis=0, keepdims=True))
            p = jnp.exp2(logit - m_new).astype(_BF16)
            acc_ref[s] = jnp.exp2(m_old - m_new) * acc_ref[s] + _dot(ckvt, p)
            m_ref[s] = m_new

    def attend_wide(c, _):
        attend_rows(pl.multiple_of(c * ATTN_ROWS, ATTN_ROWS), ATTN_ROWS)
        return 0

    per_wide = ATTN_ROWS // K_CHUNK
    lax.fori_loop(0, n_chunks // per_wide, attend_wide, 0)
    for r in range(1, per_wide):
        @pl.when(n_chunks % per_wide >= r)
        def _(r=r):
            attend_rows(chunk_start(n_chunks - n_chunks % per_wide + r - 1), K_CHUNK)

    for s in seqs:
        o_lat_t = (acc_ref[s, :KV_LATENT, :] / acc_ref[s, KV_LATENT:KV_LATENT + 1, :]).astype(_BF16)
        out_t = [_dot(wuvt_ref[hd], o_lat_t[:, hd * Q_TILE:(hd + 1) * Q_TILE]) for hd in range(N_HEADS)]
        o_ref[s] = jnp.concatenate(out_t, axis=0).T.astype(_BF16)


def _attn(qlt, qit, wit, ki, ckv, ckvt, wuvt, k_sel):
    batch, _, seq = ckvt.shape
    sp = SEQ_PER_STEP
    return pl.pallas_call(
        functools.partial(_attn_kernel, k_sel=k_sel),
        grid=(batch // sp, seq // Q_TILE),
        in_specs=[
            pl.BlockSpec((sp, 1, KV_LATENT, N_HEADS * Q_TILE), lambda b, i: (b, i, 0, 0)),
            pl.BlockSpec((sp, IDX_HEADS * IDX_DIM, Q_TILE), lambda b, i: (b, 0, i)),
            pl.BlockSpec((sp, IDX_HEADS, Q_TILE), lambda b, i: (b, 0, i)),
            pl.BlockSpec((sp, seq, IDX_DIM), lambda b, i: (b, 0, 0)),
            pl.BlockSpec((sp, seq, KV_LATENT), lambda b, i: (b, 0, 0)),
            pl.BlockSpec((sp, KV_ROWS, seq), lambda b, i: (b, 0, 0)),
            _const_spec((N_HEADS, HEAD_DIM, KV_LATENT)),
        ],
        out_specs=pl.BlockSpec((sp, Q_TILE, N_HEADS * HEAD_DIM), lambda b, i: (b, i, 0)),
        out_shape=jax.ShapeDtypeStruct((batch, seq, N_HEADS * HEAD_DIM), _BF16),
        scratch_shapes=[pltpu.VMEM((sp, seq, Q_TILE), jnp.int32),
                        pltpu.VMEM((sp, seq // K_CHUNK, WORD_BITS, SUBLANES, Q_TILE), jnp.int32),
                        pltpu.VMEM((sp, seq // K_CHUNK, SUBLANES, Q_TILE), jnp.int32),
                        pltpu.VMEM((sp, 1, Q_TILE), jnp.int32), pltpu.VMEM((sp, 1, Q_TILE), jnp.int32),
                        pltpu.VMEM((sp, 1, Q_TILE), jnp.int32), pltpu.VMEM((sp, 1, N_HEADS * Q_TILE), _F32),
                        pltpu.VMEM((sp, KV_ROWS, N_HEADS * Q_TILE), _F32)],
        compiler_params=pltpu.CompilerParams(dimension_semantics=("arbitrary", "arbitrary"),
                                             vmem_limit_bytes=VMEM_LIMIT_BYTES),
        name="dsa_attn",
    )(qlt, qit, wit, ki, ckv, ckvt, wuvt)


def _merge_kernel(x_ref, attn_ref, pool_ref, halo_ref, sga_ref, sgb_ref, wba_ref, pw_ref, ps_ref, wbp_ref, wo_ref,
                  o_ref, *, tiles_per_seq):
    tile = pl.program_id(0) % tiles_per_seq
    cur = pool_ref[...]
    halo = jnp.where(tile == 0, 0.0, halo_ref[...])
    ext = jnp.concatenate([halo, cur], axis=0)
    t1 = (tile * TOKEN_TILE + 1 + lax.broadcasted_iota(jnp.int32, (TOKEN_TILE, 1), 0)).astype(_F32)
    mixed = []
    win_sum, width = ext, 1
    for g, w in enumerate(POOL_WINDOWS):
        while width < w:
            win_sum = win_sum + pltpu.roll(win_sum, width, 0)
            width *= 2
        cols = slice(g * POOL_GROUP, (g + 1) * POOL_GROUP)
        pooled = win_sum[POOL_HALO:, cols] / jnp.minimum(t1, float(w)) - cur[:, cols]
        mixed.append(_dot(pooled.astype(_BF16), pw_ref[g]))
    mixed = (jnp.concatenate(mixed, axis=-1) * ps_ref[...]).astype(_BF16)
    merged = (sga_ref[...].astype(_F32) * _dot(attn_ref[...], wba_ref[...])
              + sgb_ref[...].astype(_F32) * _dot(mixed, wbp_ref[...]))
    o_ref[...] = x_ref[...] + _dot(merged.astype(_BF16), wo_ref[...])


def _merge(x, attn, pool, sga, sgb, wba, pw, ps, wbp, wo, seq):
    n = x.shape[0]
    tps = seq // TOKEN_TILE
    halo_blocks = TOKEN_TILE // POOL_HALO

    def flat(width):
        return pl.BlockSpec((TOKEN_TILE, width), lambda i: (i, 0))

    return pl.pallas_call(
        functools.partial(_merge_kernel, tiles_per_seq=tps),
        grid=(n // TOKEN_TILE,),
        in_specs=[
            flat(D_MODEL), flat(N_HEADS * HEAD_DIM), flat(POOL_WIDTH),
            pl.BlockSpec((POOL_HALO, POOL_WIDTH), lambda i: (jnp.maximum(i * halo_blocks - 1, 0), 0)),
            flat(D_MODEL), flat(D_MODEL),
            _const_spec((N_HEADS * HEAD_DIM, D_MODEL)), _const_spec((len(POOL_WINDOWS), POOL_GROUP, POOL_GROUP)),
            _const_spec((1, POOL_WIDTH)), _const_spec((POOL_WIDTH, D_MODEL)), _const_spec((D_MODEL, D_MODEL)),
        ],
        out_specs=flat(D_MODEL),
        out_shape=jax.ShapeDtypeStruct((n, D_MODEL), _F32),
        compiler_params=pltpu.CompilerParams(dimension_semantics=("arbitrary",), vmem_limit_bytes=VMEM_LIMIT_BYTES),
        name="merge_out",
    )(x, attn, pool, pool, sga, sgb, wba, pw, ps, wbp, wo)


def _split_w_in(w):
    cuts, off = {}, 0
    for name, width in (("q", N_HEADS * HEAD_DIM), ("ckv", KV_LATENT), ("qi", IDX_HEADS * IDX_DIM), ("ki", IDX_DIM),
                        ("wi", IDX_HEADS), ("pool", POOL_WIDTH), ("ga", D_MODEL), ("gb", D_MODEL)):
        cuts[name] = w[:, off:off + width]
        off += width
    pad_n = jnp.zeros((w.shape[0], LANES - IDX_DIM), w.dtype)
    wn = jnp.concatenate([cuts["ckv"], cuts["pool"], cuts["ga"], cuts["gb"], cuts["ki"], pad_n], axis=1)
    pad_t = jnp.zeros((w.shape[0], BF16_SUBLANES - IDX_HEADS), w.dtype)
    wt = jnp.concatenate([cuts["q"], cuts["qi"], cuts["wi"], pad_t], axis=1).T
    return wn.astype(_BF16), wt.astype(_BF16)


def kernel(x, norm_ffn1, ffn1_gate, ffn1_up, ffn1_down, norm_mix, w_in, norm_kv, w_uk, w_uv, pool_w, pool_scale,
           w_branch_attn, w_branch_pool, w_out, norm_ffn2, ffn2_gate, ffn2_up, ffn2_down, norm_final):
    batch, seq, _ = x.shape
    depth = norm_ffn1.shape[0]
    assert seq % TOKEN_TILE == 0 and seq % Q_TILE == 0 and batch % SEQ_PER_STEP == 0
    k_sel = min(TOPK_MAX, seq // 4)
    n = batch * seq
    bf = lambda a: a.astype(_BF16)
    row = lambda a: a.reshape(1, -1)
    gf = row(norm_final)
    h = x.reshape(n, D_MODEL)
    for i in range(depth):
        h = _ffn(h, row(norm_ffn1[i]), bf(ffn1_gate[i]), bf(ffn1_up[i]), bf(ffn1_down[i]), gf, False)
        wn, wt = _split_w_in(w_in[i])
        wuk_h = bf(jnp.transpose(w_uk[i], (1, 0, 2)))
        wuvt_h = bf(jnp.transpose(w_uv[i], (1, 2, 0)))
        qlt, ckv, ckvt, qit, ki, wit, pool, sga, sgb = _proj(h, row(norm_mix[i]), wn, wt, row(norm_kv[i]), wuk_h,
                                                             batch, seq)
        attn = _attn(qlt, qit, wit, ki.reshape(batch, seq, -1), ckv.reshape(batch, seq, -1), ckvt, wuvt_h, k_sel)
        h = _merge(h, attn.reshape(n, -1), pool, sga, sgb, bf(w_branch_attn[i]), bf(pool_w[i]), row(pool_scale[i]),
                   bf(w_branch_pool[i]), bf(w_out[i]), seq)
        h = _ffn(h, row(norm_ffn2[i]), bf(ffn2_gate[i]), bf(ffn2_up[i]), bf(ffn2_down[i]), gf, i == depth - 1)
    return h.reshape(batch, seq, D_MODEL)
```

```python
import functools

import jax
import jax.numpy as jnp
from jax import lax
from jax.experimental import pallas as pl
from jax.experimental.pallas import tpu as pltpu

D_MODEL = 1024
N_HEADS = 8
HEAD_DIM = 64
KV_LATENT = 128
ATTN_SCALE = HEAD_DIM ** -0.5
LOG2_E = 1.4426950408889634
IDX_HEADS = 8
IDX_DIM = 64
IDX_SCALE = (IDX_HEADS ** -0.5) * (IDX_DIM ** -0.5)
TOPK_MAX = 256
POOL_WINDOWS = (2, 4, 8, 16)
POOL_GROUP = 128
POOL_WIDTH = POOL_GROUP * len(POOL_WINDOWS)
POOL_HALO = 16
D_FF = 2816
EPS = 1e-6

LANES = 128
SUBLANES = 8
WORD_BITS = 32
BF16_SUBLANES = 16
KV_ROWS = KV_LATENT + BF16_SUBLANES
VMEM_LIMIT_BYTES = 56 * 1024 * 1024

TOKEN_TILE = 1024
FF_CHUNK = 256
Q_TILE = 256
K_CHUNK = SUBLANES * WORD_BITS
assert K_CHUNK == Q_TILE
ATTN_ROWS = 2 * K_CHUNK
SEQ_PER_STEP = 4

_N_CKV = 0
_N_POOL = _N_CKV + KV_LATENT
_N_GA = _N_POOL + POOL_WIDTH
_N_GB = _N_GA + D_MODEL
_N_KI = _N_GB + D_MODEL
_N_TOTAL = _N_KI + LANES
_T_Q = 0
_T_QI = _T_Q + N_HEADS * HEAD_DIM
_T_WI = _T_QI + IDX_HEADS * IDX_DIM
_T_TOTAL = _T_WI + BF16_SUBLANES

_F32 = jnp.float32
_BF16 = jnp.bfloat16
_INT_MIN = -(2 ** 31)
_NEG_BIG = float(jnp.finfo(jnp.float32).min)


def _const_spec(shape):
    return pl.BlockSpec(shape, lambda *_: (0,) * len(shape), pipeline_mode=pl.Buffered(1))


def _rms(x, g):
    return x * lax.rsqrt(jnp.mean(x * x, axis=-1, keepdims=True) + EPS) * g


def _dot(a, b):
    return jnp.dot(a, b, preferred_element_type=_F32)


def _ffn_kernel(x_ref, g_ref, wg_ref, wu_ref, wd_ref, gf_ref, o_ref, *, final_norm):
    x = x_ref[...]
    h = _rms(x, g_ref[...]).astype(_BF16)
    acc = jnp.zeros(x.shape, _F32)
    for c in range(D_FF // FF_CHUNK):
        sl = slice(c * FF_CHUNK, (c + 1) * FF_CHUNK)
        gate = _dot(h, wg_ref[:, sl])
        up = _dot(h, wu_ref[:, sl])
        act = (gate * jax.nn.sigmoid(gate) * up).astype(_BF16)
        acc = acc + _dot(act, wd_ref[sl, :])
    y = x + 0.5 * acc
    if final_norm:
        y = _rms(y, gf_ref[...])
    o_ref[...] = y


def _ffn(x, g, wg, wu, wd, gf, final_norm):
    n = x.shape[0]
    tile = pl.BlockSpec((TOKEN_TILE, D_MODEL), lambda i: (i, 0))
    return pl.pallas_call(
        functools.partial(_ffn_kernel, final_norm=final_norm),
        grid=(n // TOKEN_TILE,),
        in_specs=[tile, _const_spec((1, D_MODEL)), _const_spec((D_MODEL, D_FF)), _const_spec((D_MODEL, D_FF)),
                  _const_spec((D_FF, D_MODEL)), _const_spec((1, D_MODEL))],
        out_specs=tile,
        out_shape=jax.ShapeDtypeStruct((n, D_MODEL), _F32),
        compiler_params=pltpu.CompilerParams(dimension_semantics=("arbitrary",), vmem_limit_bytes=VMEM_LIMIT_BYTES),
        name="ffn",
    )(x, g, wg, wu, wd, gf)


def _proj_kernel(x_ref, g_ref, wn_ref, wt_ref, gkv_ref, wuk_ref, qlt_ref, ckv_ref, ckvt_ref, qit_ref, ki_ref, wit_ref,
                 pool_ref, sga_ref, sgb_ref):
    h32 = _rms(x_ref[...], g_ref[...])
    h = h32.astype(_BF16)
    ht = h32.T.astype(_BF16)

    def z(lo, hi):
        return _dot(h, wn_ref[:, lo:hi])

    def zt(lo, hi):
        return _dot(wt_ref[lo:hi, :], ht)

    ckv = _rms(z(_N_CKV, _N_POOL), gkv_ref[...])
    ckv_ref[...] = ckv.astype(_BF16)
    ones_tile = (lax.broadcasted_iota(jnp.int32, (BF16_SUBLANES, TOKEN_TILE), 0) == 0).astype(_F32)
    ckvt_ref[0] = jnp.concatenate([ckv.T, ones_tile], axis=0).astype(_BF16)
    pool_ref[...] = z(_N_POOL, _N_GA)
    sga_ref[...] = jax.nn.sigmoid(z(_N_GA, _N_GB)).astype(_BF16)
    sgb_ref[...] = jax.nn.sigmoid(z(_N_GB, _N_KI)).astype(_BF16)
    ki_ref[...] = z(_N_KI, _N_TOTAL)[:, :IDX_DIM].astype(_BF16)

    qt = zt(_T_Q, _T_QI).astype(_BF16)
    for hd in range(N_HEADS):
        qlt = (_dot(wuk_ref[hd], qt[hd * HEAD_DIM:(hd + 1) * HEAD_DIM, :]) * (ATTN_SCALE * LOG2_E)).astype(_BF16)
        for j in range(TOKEN_TILE // Q_TILE):
            qlt_ref[0, j, :, hd * Q_TILE:(hd + 1) * Q_TILE] = qlt[:, j * Q_TILE:(j + 1) * Q_TILE]
    qit_ref[0] = zt(_T_QI, _T_WI).astype(_BF16)
    wit_ref[0] = zt(_T_WI, _T_TOTAL)[:IDX_HEADS, :] * IDX_SCALE


def _proj(x, g, wn, wt, gkv, wuk, batch, seq):
    n = x.shape[0]
    tps = seq // TOKEN_TILE

    def flat(width):
        return pl.BlockSpec((TOKEN_TILE, width), lambda i: (i, 0))

    def feat(rows):
        return pl.BlockSpec((1, rows, TOKEN_TILE), lambda i: (i // tps, 0, i % tps))

    out_shape = [
        jax.ShapeDtypeStruct((batch, seq // Q_TILE, KV_LATENT, N_HEADS * Q_TILE), _BF16),
        jax.ShapeDtypeStruct((n, KV_LATENT), _BF16),
        jax.ShapeDtypeStruct((batch, KV_ROWS, seq), _BF16),
        jax.ShapeDtypeStruct((batch, IDX_HEADS * IDX_DIM, seq), _BF16),
        jax.ShapeDtypeStruct((n, IDX_DIM), _BF16),
        jax.ShapeDtypeStruct((batch, IDX_HEADS, seq), _F32),
        jax.ShapeDtypeStruct((n, POOL_WIDTH), _F32),
        jax.ShapeDtypeStruct((n, D_MODEL), _BF16),
        jax.ShapeDtypeStruct((n, D_MODEL), _BF16),
    ]
    out_specs = [
        pl.BlockSpec((1, TOKEN_TILE // Q_TILE, KV_LATENT, N_HEADS * Q_TILE), lambda i: (i // tps, i % tps, 0, 0)),
        flat(KV_LATENT), feat(KV_ROWS), feat(IDX_HEADS * IDX_DIM), flat(IDX_DIM), feat(IDX_HEADS),
        flat(POOL_WIDTH), flat(D_MODEL), flat(D_MODEL),
    ]
    return pl.pallas_call(
        _proj_kernel,
        grid=(n // TOKEN_TILE,),
        in_specs=[flat(D_MODEL), _const_spec((1, D_MODEL)), _const_spec((D_MODEL, _N_TOTAL)),
                  _const_spec((_T_TOTAL, D_MODEL)), _const_spec((1, KV_LATENT)),
                  _const_spec((N_HEADS, KV_LATENT, HEAD_DIM))],
        out_specs=out_specs,
        out_shape=out_shape,
        compiler_params=pltpu.CompilerParams(dimension_semantics=("arbitrary",), vmem_limit_bytes=VMEM_LIMIT_BYTES),
        name="mix_proj",
    )(x, g, wn, wt, gkv, wuk)


def _bit_planes(u):
    a = [u[j * SUBLANES:(j + 1) * SUBLANES, :] for j in range(WORD_BITS)]
    j, m = WORD_BITS // 2, 0x0000FFFF
    while j:
        for k in range(WORD_BITS):
            if k & j == 0:
                t = (a[k] ^ (a[k + j] >> j)) & m
                a[k] = a[k] ^ t
                a[k + j] = a[k + j] ^ (t << j)
        j >>= 1
        m = m ^ (m << j)
    return a


def _popcount_rows(words):
    return jnp.sum(lax.population_count(words), axis=0, keepdims=True)


def _attn_kernel(qlt_ref, qit_ref, wit_ref, ki_ref, ckv_ref, ckvt_ref, wuvt_ref, o_ref, keys_ref, planes_ref, eq_ref,
                 thr_ref, ngt_ref, neq_ref, m_ref, acc_ref, *, k_sel):
    qb = pl.program_id(1)
    n_chunks = qb + 1
    q_pos = qb * Q_TILE + lax.broadcasted_iota(jnp.int32, (1, Q_TILE), 1)
    k_eff = jnp.minimum(k_sel, q_pos + 1)
    row_iota = lax.broadcasted_iota(jnp.int32, (K_CHUNK, Q_TILE), 0)
    seqs = range(SEQ_PER_STEP)

    def chunk_start(c):
        return pl.multiple_of(c * K_CHUNK, K_CHUNK)

    def score_chunk(c, diagonal):
        k0 = chunk_start(c)
        for s in seqs:
            kc = ki_ref[s, pl.ds(k0, K_CHUNK), :]
            score = jnp.zeros((K_CHUNK, Q_TILE), _F32)
            for j in range(IDX_HEADS):
                r = _dot(kc, qit_ref[s, j * IDX_DIM:(j + 1) * IDX_DIM, :])
                score = score + jnp.maximum(r, 0.0) * wit_ref[s, j:j + 1, :]
            bits = pltpu.bitcast(score, jnp.int32)
            key = bits ^ ((bits >> 31) & 0x7FFFFFFF)
            if diagonal:
                key = jnp.where(k0 + row_iota <= q_pos, key, _INT_MIN)
            keys_ref[s, pl.ds(k0, K_CHUNK), :] = key
            for p, plane in enumerate(_bit_planes(key ^ _INT_MIN)):
                planes_ref[s, c, p] = plane

    def score_full_chunk(c, _):
        score_chunk(c, False)
        return 0

    lax.fori_loop(0, qb, score_full_chunk, 0)
    score_chunk(qb, True)

    def search(nc):
        every = jnp.full((SUBLANES, Q_TILE), -1, jnp.int32)
        none = jnp.zeros((1, Q_TILE), jnp.int32)
        width = nc + 2

        def decide_bit(p, carry):
            out = []
            for s in seqs:
                alive, n_gt, thr_u = carry[s * width:s * width + nc], carry[s * width + nc], carry[s * width + nc + 1]
                planes = [planes_ref[s, c, pl.ds(p, 1)][0] for c in range(nc)]
                ones = functools.reduce(jnp.add, [lax.population_count(a & w) for a, w in zip(alive, planes)])
                n_ones = jnp.sum(ones, axis=0, keepdims=True)
                take = n_gt + n_ones >= k_eff
                thr_u = thr_u | jnp.where(take, jnp.left_shift(jnp.int32(1), WORD_BITS - 1 - p), 0)
                n_gt = jnp.where(take, n_gt, n_gt + n_ones)
                flip = jnp.where(take, 0, -1)
                out += [a & (w ^ flip) for a, w in zip(alive, planes)] + [n_gt, thr_u]
            return tuple(out)

        res = lax.fori_loop(0, WORD_BITS, decide_bit, tuple(([every] * nc + [none, none]) * SEQ_PER_STEP))
        for s in seqs:
            alive = res[s * width:s * width + nc]
            thr_ref[s, 0:1, :] = res[s * width + nc + 1] ^ _INT_MIN
            ngt_ref[s, 0:1, :] = res[s * width + nc]
            neq_ref[s, 0:1, :] = functools.reduce(jnp.add, [_popcount_rows(a) for a in alive])
            for c in range(nc):
                eq_ref[s, c] = alive[c]

    n_max = keys_ref.shape[1] // K_CHUNK
    for nc in range(1, n_max + 1):
        @pl.when(n_chunks == nc)
        def _(nc=nc):
            if nc * K_CHUNK <= k_sel:
                for s in seqs:
                    thr_ref[s, 0:1, :] = jnp.full((1, Q_TILE), _INT_MIN + 1, jnp.int32)
                    ngt_ref[s, 0:1, :] = k_eff
                    neq_ref[s, 0:1, :] = jnp.zeros((1, Q_TILE), jnp.int32)
            else:
                search(nc)

    thr = [thr_ref[s, 0:1, :] for s in seqs]
    need = [k_eff - ngt_ref[s, 0:1, :] for s in seqs]
    surplus = [neq_ref[s, 0:1, :] - need[s] for s in seqs]

    @pl.when(functools.reduce(jnp.maximum, [jnp.max(x) for x in surplus]) > 0)
    def _():
        sublane = lax.broadcasted_iota(jnp.int32, (SUBLANES, Q_TILE), 0)
        group_bits = WORD_BITS.bit_length() - 1
        row_bits = SUBLANES.bit_length() - 1

        def ties_before(limits):
            within = []
            for lim in limits:
                group = (lim >> row_bits) & (WORD_BITS - 1)
                this_bit = jnp.left_shift(jnp.int32(1), WORD_BITS - 1 - group)
                earlier_groups = ~(jnp.left_shift(jnp.int32(2), WORD_BITS - 1 - group) - 1)
                within.append(earlier_groups | jnp.where(sublane < (lim & (SUBLANES - 1)), this_bit, 0))

            def body(c, accs):
                out = []
                for s in seqs:
                    c_lim = limits[s] >> (row_bits + group_bits)
                    mask = jnp.where(c < c_lim, -1, jnp.where(c == c_lim, within[s], 0))
                    out.append(accs[s] + lax.population_count(eq_ref[s, c] & mask))
                return tuple(out)

            zero = jnp.zeros((SUBLANES, Q_TILE), jnp.int32)
            accs = lax.fori_loop(0, n_chunks, body, (zero,) * SEQ_PER_STEP)
            return [jnp.sum(a, axis=0, keepdims=True) for a in accs]

        pos_bits = keys_ref.shape[1].bit_length()

        def grow(it, limits):
            cands = [lim + jnp.left_shift(jnp.int32(1), pos_bits - 1 - it) for lim in limits]
            n_before = ties_before(cands)
            return tuple(jnp.where(n_before[s] <= need[s], cands[s], limits[s]) for s in seqs)

        limits = lax.fori_loop(0, pos_bits, grow, (jnp.zeros((1, Q_TILE), jnp.int32),) * SEQ_PER_STEP)

        def demote(c, _):
            k0 = chunk_start(c)
            for s in seqs:
                k = keys_ref[s, pl.ds(k0, K_CHUNK), :]
                drop = (k == thr[s]) & (k0 + row_iota >= limits[s])
                keys_ref[s, pl.ds(k0, K_CHUNK), :] = jnp.where(drop, thr[s] - 1, k)
            return 0

        lax.fori_loop(0, n_chunks, demote, 0)

    m_ref[...] = jnp.full(m_ref.shape, _NEG_BIG, _F32)
    acc_ref[...] = jnp.zeros(acc_ref.shape, _F32)

    def attend_rows(k0, size):
        for first in range(0, SEQ_PER_STEP, 2):
            pair = range(first, min(first + 2, SEQ_PER_STEP))
            logits, m_news = {}, {}
            for s in pair:
                bias = jnp.where(keys_ref[s, pl.ds(k0, size), :] >= thr[s], 0.0, _NEG_BIG)
                logits[s] = (_dot(ckv_ref[s, pl.ds(k0, size), :], qlt_ref[s, 0])
                             + jnp.concatenate([bias] * N_HEADS, axis=1))
                m_news[s] = jnp.maximum(m_ref[s], jnp.max(logits[s], axis=0, keepdims=True))
            for s in pair:
                p = jnp.exp2(logits[s] - m_news[s]).astype(_BF16)
                acc_ref[s] = jnp.exp2(m_ref[s] - m_news[s]) * acc_ref[s] + _dot(ckvt_ref[s, :, pl.ds(k0, size)], p)
                m_ref[s] = m_news[s]

    def attend_wide(c, _):
        attend_rows(pl.multiple_of(c * ATTN_ROWS, ATTN_ROWS), ATTN_ROWS)
        return 0

    per_wide = ATTN_ROWS // K_CHUNK
    lax.fori_loop(0, n_chunks // per_wide, attend_wide, 0)
    for r in range(1, per_wide):
        @pl.when(n_chunks % per_wide >= r)
        def _(r=r):
            attend_rows(chunk_start(n_chunks - n_chunks % per_wide + r - 1), K_CHUNK)

    for s in seqs:
        o_lat_t = (acc_ref[s, :KV_LATENT, :] / acc_ref[s, KV_LATENT:KV_LATENT + 1, :]).astype(_BF16)
        out_t = [_dot(wuvt_ref[hd], o_lat_t[:, hd * Q_TILE:(hd + 1) * Q_TILE]) for hd in range(N_HEADS)]
        o_ref[s] = jnp.concatenate(out_t, axis=0).T.astype(_BF16)


def _attn(qlt, qit, wit, ki, ckv, ckvt, wuvt, k_sel):
    batch, _, seq = ckvt.shape
    sp = SEQ_PER_STEP
    return pl.pallas_call(
        functools.partial(_attn_kernel, k_sel=k_sel),
        grid=(batch // sp, seq // Q_TILE),
        in_specs=[
            pl.BlockSpec((sp, 1, KV_LATENT, N_HEADS * Q_TILE), lambda b, i: (b, i, 0, 0)),
            pl.BlockSpec((sp, IDX_HEADS * IDX_DIM, Q_TILE), lambda b, i: (b, 0, i)),
            pl.BlockSpec((sp, IDX_HEADS, Q_TILE), lambda b, i: (b, 0, i)),
            pl.BlockSpec((sp, seq, IDX_DIM), lambda b, i: (b, 0, 0)),
            pl.BlockSpec((sp, seq, KV_LATENT), lambda b, i: (b, 0, 0)),
            pl.BlockSpec((sp, KV_ROWS, seq), lambda b, i: (b, 0, 0)),
            _const_spec((N_HEADS, HEAD_DIM, KV_LATENT)),
        ],
        out_specs=pl.BlockSpec((sp, Q_TILE, N_HEADS * HEAD_DIM), lambda b, i: (b, i, 0)),
        out_shape=jax.ShapeDtypeStruct((batch, seq, N_HEADS * HEAD_DIM), _BF16),
        scratch_shapes=[pltpu.VMEM((sp, seq, Q_TILE), jnp.int32),
                        pltpu.VMEM((sp, seq // K_CHUNK, WORD_BITS, SUBLANES, Q_TILE), jnp.int32),
                        pltpu.VMEM((sp, seq // K_CHUNK, SUBLANES, Q_TILE), jnp.int32),
                        pltpu.VMEM((sp, SUBLANES, Q_TILE), jnp.int32), pltpu.VMEM((sp, SUBLANES, Q_TILE), jnp.int32),
                        pltpu.VMEM((sp, SUBLANES, Q_TILE), jnp.int32), pltpu.VMEM((sp, 1, N_HEADS * Q_TILE), _F32),
                        pltpu.VMEM((sp, KV_ROWS, N_HEADS * Q_TILE), _F32)],
        compiler_params=pltpu.CompilerParams(dimension_semantics=("arbitrary", "arbitrary"),
                                             vmem_limit_bytes=VMEM_LIMIT_BYTES),
        name="dsa_attn",
    )(qlt, qit, wit, ki, ckv, ckvt, wuvt)


def _merge_kernel(x_ref, attn_ref, pool_ref, halo_ref, sga_ref, sgb_ref, wba_ref, pw_ref, ps_ref, wbp_ref, wo_ref,
                  o_ref, *, tiles_per_seq):
    tile = pl.program_id(0) % tiles_per_seq
    cur = pool_ref[...]
    halo = jnp.where(tile == 0, 0.0, halo_ref[...])
    ext = jnp.concatenate([halo, cur], axis=0)
    t1 = (tile * TOKEN_TILE + 1 + lax.broadcasted_iota(jnp.int32, (TOKEN_TILE, 1), 0)).astype(_F32)
    mixed = []
    win_sum, width = ext, 1
    for g, w in enumerate(POOL_WINDOWS):
        while width < w:
            win_sum = win_sum + pltpu.roll(win_sum, width, 0)
            width *= 2
        cols = slice(g * POOL_GROUP, (g + 1) * POOL_GROUP)
        pooled = win_sum[POOL_HALO:, cols] / jnp.minimum(t1, float(w)) - cur[:, cols]
        mixed.append(_dot(pooled.astype(_BF16), pw_ref[g]))
    mixed = (jnp.concatenate(mixed, axis=-1) * ps_ref[...]).astype(_BF16)
    merged = (sga_ref[...].astype(_F32) * _dot(attn_ref[...], wba_ref[...])
              + sgb_ref[...].astype(_F32) * _dot(mixed, wbp_ref[...]))
    o_ref[...] = x_ref[...] + _dot(merged.astype(_BF16), wo_ref[...])


def _merge(x, attn, pool, sga, sgb, wba, pw, ps, wbp, wo, seq):
    n = x.shape[0]
    tps = seq // TOKEN_TILE
    halo_blocks = TOKEN_TILE // POOL_HALO

    def flat(width):
        return pl.BlockSpec((TOKEN_TILE, width), lambda i: (i, 0))

    return pl.pallas_call(
        functools.partial(_merge_kernel, tiles_per_seq=tps),
        grid=(n // TOKEN_TILE,),
        in_specs=[
            flat(D_MODEL), flat(N_HEADS * HEAD_DIM), flat(POOL_WIDTH),
            pl.BlockSpec((POOL_HALO, POOL_WIDTH), lambda i: (jnp.maximum(i * halo_blocks - 1, 0), 0)),
            flat(D_MODEL), flat(D_MODEL),
            _const_spec((N_HEADS * HEAD_DIM, D_MODEL)), _const_spec((len(POOL_WINDOWS), POOL_GROUP, POOL_GROUP)),
            _const_spec((1, POOL_WIDTH)), _const_spec((POOL_WIDTH, D_MODEL)), _const_spec((D_MODEL, D_MODEL)),
        ],
        out_specs=flat(D_MODEL),
        out_shape=jax.ShapeDtypeStruct((n, D_MODEL), _F32),
        compiler_params=pltpu.CompilerParams(dimension_semantics=("arbitrary",), vmem_limit_bytes=VMEM_LIMIT_BYTES),
        name="merge_out",
    )(x, attn, pool, pool, sga, sgb, wba, pw, ps, wbp, wo)


def _split_w_in(w):
    cuts, off = {}, 0
    for name, width in (("q", N_HEADS * HEAD_DIM), ("ckv", KV_LATENT), ("qi", IDX_HEADS * IDX_DIM), ("ki", IDX_DIM),
                        ("wi", IDX_HEADS), ("pool", POOL_WIDTH), ("ga", D_MODEL), ("gb", D_MODEL)):
        cuts[name] = w[:, off:off + width]
        off += width
    pad_n = jnp.zeros((w.shape[0], LANES - IDX_DIM), w.dtype)
    wn = jnp.concatenate([cuts["ckv"], cuts["pool"], cuts["ga"], cuts["gb"], cuts["ki"], pad_n], axis=1)
    pad_t = jnp.zeros((w.shape[0], BF16_SUBLANES - IDX_HEADS), w.dtype)
    wt = jnp.concatenate([cuts["q"], cuts["qi"], cuts["wi"], pad_t], axis=1).T
    return wn.astype(_BF16), wt.astype(_BF16)


def kernel(x, norm_ffn1, ffn1_gate, ffn1_up, ffn1_down, norm_mix, w_in, norm_kv, w_uk, w_uv, pool_w, pool_scale,
           w_branch_attn, w_branch_pool, w_out, norm_ffn2, ffn2_gate, ffn2_up, ffn2_down, norm_final):
    batch, seq, _ = x.shape
    depth = norm_ffn1.shape[0]
    assert seq % TOKEN_TILE == 0 and seq % Q_TILE == 0 and batch % SEQ_PER_STEP == 0
    k_sel = min(TOPK_MAX, seq // 4)
    n = batch * seq
    bf = lambda a: a.astype(_BF16)
    row = lambda a: a.reshape(1, -1)
    gf = row(norm_final)
    h = x.reshape(n, D_MODEL)
    for i in range(depth):
        h = _ffn(h, row(norm_ffn1[i]), bf(ffn1_gate[i]), bf(ffn1_up[i]), bf(ffn1_down[i]), gf, False)
        wn, wt = _split_w_in(w_in[i])
        wuk_h = bf(jnp.transpose(w_uk[i], (1, 0, 2)))
        wuvt_h = bf(jnp.transpose(w_uv[i], (1, 2, 0)))
        qlt, ckv, ckvt, qit, ki, wit, pool, sga, sgb = _proj(h, row(norm_mix[i]), wn, wt, row(norm_kv[i]), wuk_h,
                                                             batch, seq)
        attn = _attn(qlt, qit, wit, ki.reshape(batch, seq, -1), ckv.reshape(batch, seq, -1), ckvt, wuvt_h, k_sel)
        h = _merge(h, attn.reshape(n, -1), pool, sga, sgb, bf(w_branch_attn[i]), bf(pool_w[i]), row(pool_scale[i]),
                   bf(w_branch_pool[i]), bf(w_out[i]), seq)
        h = _ffn(h, row(norm_ffn2[i]), bf(ffn2_gate[i]), bf(ffn2_up[i]), bf(ffn2_down[i]), gf, i == depth - 1)
    return h.reshape(batch, seq, D_MODEL)
```

```python
import functools

import jax
import jax.numpy as jnp
from jax import lax
from jax.experimental import pallas as pl
from jax.experimental.pallas import tpu as pltpu

D_MODEL = 1024
N_HEADS = 8
HEAD_DIM = 64
KV_LATENT = 128
ATTN_SCALE = HEAD_DIM ** -0.5
LOG2_E = 1.4426950408889634
IDX_HEADS = 8
IDX_DIM = 64
IDX_SCALE = (IDX_HEADS ** -0.5) * (IDX_DIM ** -0.5)
TOPK_MAX = 256
POOL_WINDOWS = (2, 4, 8, 16)
POOL_GROUP = 128
POOL_WIDTH = POOL_GROUP * len(POOL_WINDOWS)
POOL_HALO = 16
D_FF = 2816
EPS = 1e-6

LANES = 128
SUBLANES = 8
WORD_BITS = 32
BF16_SUBLANES = 16
KV_ROWS = KV_LATENT + BF16_SUBLANES
VMEM_LIMIT_BYTES = 56 * 1024 * 1024

TOKEN_TILE = 1024
FF_CHUNK = 256
Q_TILE = 256
K_CHUNK = SUBLANES * WORD_BITS
assert K_CHUNK == Q_TILE
ATTN_ROWS = 2 * K_CHUNK
SEQ_PER_STEP = 4

_N_CKV = 0
_N_POOL = _N_CKV + KV_LATENT
_N_GA = _N_POOL + POOL_WIDTH
_N_GB = _N_GA + D_MODEL
_N_KI = _N_GB + D_MODEL
_N_TOTAL = _N_KI + LANES
_T_Q = 0
_T_QI = _T_Q + N_HEADS * HEAD_DIM
_T_WI = _T_QI + IDX_HEADS * IDX_DIM
_T_TOTAL = _T_WI + BF16_SUBLANES

_F32 = jnp.float32
_BF16 = jnp.bfloat16
_INT_MIN = -(2 ** 31)
_NEG_BIG = float(jnp.finfo(jnp.float32).min)


def _const_spec(shape):
    return pl.BlockSpec(shape, lambda *_: (0,) * len(shape), pipeline_mode=pl.Buffered(1))


def _rms(x, g):
    return x * lax.rsqrt(jnp.mean(x * x, axis=-1, keepdims=True) + EPS) * g


def _dot(a, b):
    return jnp.dot(a, b, preferred_element_type=_F32)


def _ffn_kernel(x_ref, g_ref, wg_ref, wu_ref, wd_ref, gf_ref, o_ref, *, final_norm):
    x = x_ref[...]
    h = _rms(x, g_ref[...]).astype(_BF16)
    acc = jnp.zeros(x.shape, _F32)
    for c in range(D_FF // FF_CHUNK):
        sl = slice(c * FF_CHUNK, (c + 1) * FF_CHUNK)
        gate = _dot(h, wg_ref[:, sl])
        up = _dot(h, wu_ref[:, sl])
        act = (gate * jax.nn.sigmoid(gate) * up).astype(_BF16)
        acc = acc + _dot(act, wd_ref[sl, :])
    y = x + 0.5 * acc
    if final_norm:
        y = _rms(y, gf_ref[...])
    o_ref[...] = y


def _ffn(x, g, wg, wu, wd, gf, final_norm):
    n = x.shape[0]
    tile = pl.BlockSpec((TOKEN_TILE, D_MODEL), lambda i: (i, 0))
    return pl.pallas_call(
        functools.partial(_ffn_kernel, final_norm=final_norm),
        grid=(n // TOKEN_TILE,),
        in_specs=[tile, _const_spec((1, D_MODEL)), _const_spec((D_MODEL, D_FF)), _const_spec((D_MODEL, D_FF)),
                  _const_spec((D_FF, D_MODEL)), _const_spec((1, D_MODEL))],
        out_specs=tile,
        out_shape=jax.ShapeDtypeStruct((n, D_MODEL), _F32),
        compiler_params=pltpu.CompilerParams(dimension_semantics=("arbitrary",), vmem_limit_bytes=VMEM_LIMIT_BYTES),
        name="ffn",
    )(x, g, wg, wu, wd, gf)


def _proj_kernel(x_ref, g_ref, wn_ref, wt_ref, gkv_ref, wuk_ref, qlt_ref, ckv_ref, ckvt_ref, qit_ref, ki_ref, wit_ref,
                 pool_ref, sga_ref, sgb_ref):
    h32 = _rms(x_ref[...], g_ref[...])
    h = h32.astype(_BF16)
    ht = h32.T.astype(_BF16)

    def z(lo, hi):
        return _dot(h, wn_ref[:, lo:hi])

    def zt(lo, hi):
        return _dot(wt_ref[lo:hi, :], ht)

    ckv = _rms(z(_N_CKV, _N_POOL), gkv_ref[...])
    ckv_ref[...] = ckv.astype(_BF16)
    ones_tile = (lax.broadcasted_iota(jnp.int32, (BF16_SUBLANES, TOKEN_TILE), 0) == 0).astype(_F32)
    ckvt_ref[0] = jnp.concatenate([ckv.T, ones_tile], axis=0).astype(_BF16)
    pool_ref[...] = z(_N_POOL, _N_GA)
    sga_ref[...] = jax.nn.sigmoid(z(_N_GA, _N_GB)).astype(_BF16)
    sgb_ref[...] = jax.nn.sigmoid(z(_N_GB, _N_KI)).astype(_BF16)
    ki_ref[...] = z(_N_KI, _N_TOTAL)[:, :IDX_DIM].astype(_BF16)

    qt = zt(_T_Q, _T_QI).astype(_BF16)
    for hd in range(N_HEADS):
        qlt = (_dot(wuk_ref[hd], qt[hd * HEAD_DIM:(hd + 1) * HEAD_DIM, :]) * (ATTN_SCALE * LOG2_E)).astype(_BF16)
        for j in range(TOKEN_TILE // Q_TILE):
            qlt_ref[0, j, :, hd * Q_TILE:(hd + 1) * Q_TILE] = qlt[:, j * Q_TILE:(j + 1) * Q_TILE]
    qit_ref[0] = zt(_T_QI, _T_WI).astype(_BF16)
    wit_ref[0] = zt(_T_WI, _T_TOTAL)[:IDX_HEADS, :] * IDX_SCALE


def _proj(x, g, wn, wt, gkv, wuk, batch, seq):
    n = x.shape[0]
    tps = seq // TOKEN_TILE

    def flat(width):
        return pl.BlockSpec((TOKEN_TILE, width), lambda i: (i, 0))

    def feat(rows):
        return pl.BlockSpec((1, rows, TOKEN_TILE), lambda i: (i // tps, 0, i % tps))

    out_shape = [
        jax.ShapeDtypeStruct((batch, seq // Q_TILE, KV_LATENT, N_HEADS * Q_TILE), _BF16),
        jax.ShapeDtypeStruct((n, KV_LATENT), _BF16),
        jax.ShapeDtypeStruct((batch, KV_ROWS, seq), _BF16),
        jax.ShapeDtypeStruct((batch, IDX_HEADS * IDX_DIM, seq), _BF16),
        jax.ShapeDtypeStruct((n, IDX_DIM), _BF16),
        jax.ShapeDtypeStruct((batch, IDX_HEADS, seq), _F32),
        jax.ShapeDtypeStruct((n, POOL_WIDTH), _F32),
        jax.ShapeDtypeStruct((n, D_MODEL), _BF16),
        jax.ShapeDtypeStruct((n, D_MODEL), _BF16),
    ]
    out_specs = [
        pl.BlockSpec((1, TOKEN_TILE // Q_TILE, KV_LATENT, N_HEADS * Q_TILE), lambda i: (i // tps, i % tps, 0, 0)),
        flat(KV_LATENT), feat(KV_ROWS), feat(IDX_HEADS * IDX_DIM), flat(IDX_DIM), feat(IDX_HEADS),
        flat(POOL_WIDTH), flat(D_MODEL), flat(D_MODEL),
    ]
    return pl.pallas_call(
        _proj_kernel,
        grid=(n // TOKEN_TILE,),
        in_specs=[flat(D_MODEL), _const_spec((1, D_MODEL)), _const_spec((D_MODEL, _N_TOTAL)),
                  _const_spec((_T_TOTAL, D_MODEL)), _const_spec((1, KV_LATENT)),
                  _const_spec((N_HEADS, KV_LATENT, HEAD_DIM))],
        out_specs=out_specs,
        out_shape=out_shape,
        compiler_params=pltpu.CompilerParams(dimension_semantics=("arbitrary",), vmem_limit_bytes=VMEM_LIMIT_BYTES),
        name="mix_proj",
    )(x, g, wn, wt, gkv, wuk)


def _bit_planes(u):
    a = [u[j * SUBLANES:(j + 1) * SUBLANES, :] for j in range(WORD_BITS)]
    j, m = WORD_BITS // 2, 0x0000FFFF
    while j:
        for k in range(WORD_BITS):
            if k & j == 0:
                t = (a[k] ^ (a[k + j] >> j)) & m
                a[k] = a[k] ^ t
                a[k + j] = a[k + j] ^ (t << j)
        j >>= 1
        m = m ^ (m << j)
    return a


def _popcount_rows(words):
    return jnp.sum(lax.population_count(words), axis=0, keepdims=True)


def _attn_kernel(qlt_ref, qit_ref, wit_ref, qit_next_ref, wit_next_ref, ki_ref, ckv_ref, ckvt_ref, wuvt_ref, o_ref,
                 planes_ref, sel_ref, eq_ref, ngt_ref, neq_ref, m_ref, acc_ref, *, k_sel):
    qb = pl.program_id(1)
    n_tiles = pl.num_programs(1)
    n_chunks = qb + 1
    slot = qb % 2
    q_pos = qb * Q_TILE + lax.broadcasted_iota(jnp.int32, (1, Q_TILE), 1)
    k_eff = jnp.minimum(k_sel, q_pos + 1)
    row_iota = lax.broadcasted_iota(jnp.int32, (K_CHUNK, Q_TILE), 0)
    lane_iota_rows = lax.broadcasted_iota(jnp.int32, (K_CHUNK, Q_TILE), 1)
    seqs = range(SEQ_PER_STEP)

    def chunk_start(c):
        return pl.multiple_of(c * K_CHUNK, K_CHUNK)

    def score_chunk(c, qit, wit, dst, diagonal):
        k0 = chunk_start(c)
        for s in seqs:
            kc = ki_ref[s, pl.ds(k0, K_CHUNK), :]
            score = jnp.zeros((K_CHUNK, Q_TILE), _F32)
            for j in range(IDX_HEADS):
                r = _dot(kc, qit[s, j * IDX_DIM:(j + 1) * IDX_DIM, :])
                score = score + jnp.maximum(r, 0.0) * wit[s, j:j + 1, :]
            bits = pltpu.bitcast(score, jnp.int32)
            key = bits ^ ((bits >> 31) & 0x7FFFFFFF)
            if diagonal:
                key = jnp.where(row_iota <= lane_iota_rows, key, _INT_MIN)
            for p, plane in enumerate(_bit_planes(key ^ _INT_MIN)):
                planes_ref[dst, s, c, p] = plane

    if K_CHUNK > k_sel:
        @pl.when(qb == 0)
        def _():
            score_chunk(0, qit_ref, wit_ref, slot, True)

    def search(nc):
        every = jnp.full((SUBLANES, Q_TILE), -1, jnp.int32)
        none = jnp.zeros((1, Q_TILE), jnp.int32)
        width = 2 * nc + 1

        def decide_bit(p, carry):
            out = []
            for s in seqs:
                base = s * width
                alive, above, n_gt = carry[base:base + nc], carry[base + nc:base + 2 * nc], carry[base + 2 * nc]
                planes = [planes_ref[slot, s, c, pl.ds(p, 1)][0] for c in range(nc)]
                ones = [a & w for a, w in zip(alive, planes)]
                n_ones = jnp.sum(functools.reduce(jnp.add, [lax.population_count(x) for x in ones]), axis=0,
                                 keepdims=True)
                take = n_gt + n_ones >= k_eff
                n_gt = jnp.where(take, n_gt, n_gt + n_ones)
                flip = jnp.where(take, 0, -1)
                out += [a & (w ^ flip) for a, w in zip(alive, planes)]
                out += [g | (x & flip) for g, x in zip(above, ones)] + [n_gt]
            return tuple(out)

        blank = jnp.zeros((SUBLANES, Q_TILE), jnp.int32)
        res = lax.fori_loop(0, WORD_BITS, decide_bit, tuple(([every] * nc + [blank] * nc + [none]) * SEQ_PER_STEP))
        for s in seqs:
            base = s * width
            alive, above = res[base:base + nc], res[base + nc:base + 2 * nc]
            ngt_ref[s, 0:1, :] = res[base + 2 * nc]
            neq_ref[s, 0:1, :] = functools.reduce(jnp.add, [_popcount_rows(a) for a in alive])
            for c in range(nc):
                eq_ref[s, c] = alive[c]
                sel_ref[s, c] = alive[c] | above[c]

    sub_iota = lax.broadcasted_iota(jnp.int32, (SUBLANES, Q_TILE), 0)
    lane_iota = lax.broadcasted_iota(jnp.int32, (SUBLANES, Q_TILE), 1)
    n_groups = jnp.where(lane_iota >= sub_iota, ((lane_iota - sub_iota) >> (SUBLANES.bit_length() - 1)) + 1, 0)
    causal_words = jnp.where(n_groups > 0, jnp.left_shift(jnp.int32(-1), WORD_BITS - n_groups), 0)

    n_max = planes_ref.shape[2]
    for nc in range(1, n_max + 1):
        @pl.when(n_chunks == nc)
        def _(nc=nc):
            if nc * K_CHUNK <= k_sel:
                for s in seqs:
                    for c in range(nc):
                        sel_ref[s, c] = jnp.where(c < qb, -1, causal_words)
                    ngt_ref[s, 0:1, :] = k_eff
                    neq_ref[s, 0:1, :] = jnp.zeros((1, Q_TILE), jnp.int32)
            else:
                search(nc)

    need = [k_eff - ngt_ref[s, 0:1, :] for s in seqs]
    surplus = [neq_ref[s, 0:1, :] - need[s] for s in seqs]

    @pl.when(functools.reduce(jnp.maximum, [jnp.max(x) for x in surplus]) > 0)
    def _():
        sublane = lax.broadcasted_iota(jnp.int32, (SUBLANES, Q_TILE), 0)
        group_bits = WORD_BITS.bit_length() - 1
        row_bits = SUBLANES.bit_length() - 1

        def before_limit(lim):
            group = (lim >> row_bits) & (WORD_BITS - 1)
            this_bit = jnp.left_shift(jnp.int32(1), WORD_BITS - 1 - group)
            earlier_groups = ~(jnp.left_shift(jnp.int32(2), WORD_BITS - 1 - group) - 1)
            return earlier_groups | jnp.where(sublane < (lim & (SUBLANES - 1)), this_bit, 0)

        def chunk_mask(c, lim, within):
            c_lim = lim >> (row_bits + group_bits)
            return jnp.where(c < c_lim, -1, jnp.where(c == c_lim, within, 0))

        def ties_before(limits):
            within = [before_limit(lim) for lim in limits]

            def body(c, accs):
                return tuple(accs[s] + lax.population_count(eq_ref[s, c] & chunk_mask(c, limits[s], within[s]))
                             for s in seqs)

            zero = jnp.zeros((SUBLANES, Q_TILE), jnp.int32)
            accs = lax.fori_loop(0, n_chunks, body, (zero,) * SEQ_PER_STEP)
            return [jnp.sum(a, axis=0, keepdims=True) for a in accs]

        pos_bits = (planes_ref.shape[2] * K_CHUNK).bit_length()

        def grow(it, limits):
            cands = [lim + jnp.left_shift(jnp.int32(1), pos_bits - 1 - it) for lim in limits]
            n_before = ties_before(cands)
            return tuple(jnp.where(n_before[s] <= need[s], cands[s], limits[s]) for s in seqs)

        limits = lax.fori_loop(0, pos_bits, grow, (jnp.zeros((1, Q_TILE), jnp.int32),) * SEQ_PER_STEP)

        kept = [before_limit(lim) for lim in limits]

        def demote(c, _):
            for s in seqs:
                sel_ref[s, c] = sel_ref[s, c] & (~eq_ref[s, c] | chunk_mask(c, limits[s], kept[s]))
            return 0

        lax.fori_loop(0, n_chunks, demote, 0)

    m_ref[...] = jnp.full(m_ref.shape, _NEG_BIG, _F32)
    acc_ref[...] = jnp.zeros(acc_ref.shape, _F32)

    def select_bias(s, k0, size):
        rows = []
        for i in range(size // K_CHUNK):
            words = sel_ref[s, k0 // K_CHUNK + i]
            rows += [jnp.where(jnp.left_shift(words, j) < 0, 0.0, _NEG_BIG) for j in range(WORD_BITS)]
        return jnp.concatenate(rows, axis=0)

    def attend_rows(k0, size):
        for first in range(0, SEQ_PER_STEP, 2):
            pair = range(first, min(first + 2, SEQ_PER_STEP))
            logits, m_news = {}, {}
            for s in pair:
                bias = select_bias(s, k0, size)
                logits[s] = (_dot(ckv_ref[s, pl.ds(k0, size), :], qlt_ref[s, 0])
                             + jnp.concatenate([bias] * N_HEADS, axis=1))
                m_news[s] = jnp.maximum(m_ref[s], jnp.max(logits[s], axis=0, keepdims=True))
            for s in pair:
                p = jnp.exp2(logits[s] - m_news[s]).astype(_BF16)
                acc_ref[s] = jnp.exp2(m_ref[s] - m_news[s]) * acc_ref[s] + _dot(ckvt_ref[s, :, pl.ds(k0, size)], p)
                m_ref[s] = m_news[s]

    per_wide = ATTN_ROWS // K_CHUNK

    def attend_all(score_next):
        def wide(b, _):
            attend_rows(pl.multiple_of(b * ATTN_ROWS, ATTN_ROWS), ATTN_ROWS)
            if score_next:
                for r in range(per_wide):
                    score_chunk(b * per_wide + r, qit_next_ref, wit_next_ref, 1 - slot, False)
            return 0

        lax.fori_loop(0, n_chunks // per_wide, wide, 0)
        for r in range(1, per_wide):
            @pl.when(n_chunks % per_wide >= r)
            def _(r=r):
                c = n_chunks - n_chunks % per_wide + r - 1
                attend_rows(chunk_start(c), K_CHUNK)
                if score_next:
                    score_chunk(c, qit_next_ref, wit_next_ref, 1 - slot, False)
        if score_next:
            score_chunk(qb + 1, qit_next_ref, wit_next_ref, 1 - slot, True)

    @pl.when(qb + 1 < n_tiles)
    def _():
        attend_all(True)

    @pl.when(qb + 1 == n_tiles)
    def _():
        attend_all(False)

    for s in seqs:
        o_lat_t = (acc_ref[s, :KV_LATENT, :] / acc_ref[s, KV_LATENT:KV_LATENT + 1, :]).astype(_BF16)
        out_t = [_dot(wuvt_ref[hd], o_lat_t[:, hd * Q_TILE:(hd + 1) * Q_TILE]) for hd in range(N_HEADS)]
        o_ref[s] = jnp.concatenate(out_t, axis=0).T.astype(_BF16)


def _attn(qlt, qit, wit, ki, ckv, ckvt, wuvt, k_sel):
    batch, _, seq = ckvt.shape
    sp = SEQ_PER_STEP
    last_tile = seq // Q_TILE - 1
    return pl.pallas_call(
        functools.partial(_attn_kernel, k_sel=k_sel),
        grid=(batch // sp, seq // Q_TILE),
        in_specs=[
            pl.BlockSpec((sp, 1, KV_LATENT, N_HEADS * Q_TILE), lambda b, i: (b, i, 0, 0)),
            pl.BlockSpec((sp, IDX_HEADS * IDX_DIM, Q_TILE), lambda b, i: (b, 0, i)),
            pl.BlockSpec((sp, IDX_HEADS, Q_TILE), lambda b, i: (b, 0, i)),
            pl.BlockSpec((sp, IDX_HEADS * IDX_DIM, Q_TILE), lambda b, i: (b, 0, jnp.minimum(i + 1, last_tile))),
            pl.BlockSpec((sp, IDX_HEADS, Q_TILE), lambda b, i: (b, 0, jnp.minimum(i + 1, last_tile))),
            pl.BlockSpec((sp, seq, IDX_DIM), lambda b, i: (b, 0, 0)),
            pl.BlockSpec((sp, seq, KV_LATENT), lambda b, i: (b, 0, 0)),
            pl.BlockSpec((sp, KV_ROWS, seq), lambda b, i: (b, 0, 0)),
            _const_spec((N_HEADS, HEAD_DIM, KV_LATENT)),
        ],
        out_specs=pl.BlockSpec((sp, Q_TILE, N_HEADS * HEAD_DIM), lambda b, i: (b, i, 0)),
        out_shape=jax.ShapeDtypeStruct((batch, seq, N_HEADS * HEAD_DIM), _BF16),
        scratch_shapes=[pltpu.VMEM((2, sp, seq // K_CHUNK, WORD_BITS, SUBLANES, Q_TILE), jnp.int32),
                        pltpu.VMEM((sp, seq // K_CHUNK, SUBLANES, Q_TILE), jnp.int32),
                        pltpu.VMEM((sp, seq // K_CHUNK, SUBLANES, Q_TILE), jnp.int32),
                        pltpu.VMEM((sp, SUBLANES, Q_TILE), jnp.int32), pltpu.VMEM((sp, SUBLANES, Q_TILE), jnp.int32),
                        pltpu.VMEM((sp, 1, N_HEADS * Q_TILE), _F32),
                        pltpu.VMEM((sp, KV_ROWS, N_HEADS * Q_TILE), _F32)],
        compiler_params=pltpu.CompilerParams(dimension_semantics=("arbitrary", "arbitrary"),
                                             vmem_limit_bytes=VMEM_LIMIT_BYTES),
        name="dsa_attn",
    )(qlt, qit, wit, qit, wit, ki, ckv, ckvt, wuvt)


def _merge_kernel(x_ref, attn_ref, pool_ref, halo_ref, sga_ref, sgb_ref, wba_ref, pw_ref, ps_ref, wbp_ref, wo_ref,
                  o_ref, *, tiles_per_seq):
    tile = pl.program_id(0) % tiles_per_seq
    cur = pool_ref[...]
    halo = jnp.where(tile == 0, 0.0, halo_ref[...])
    ext = jnp.concatenate([halo, cur], axis=0)
    t1 = (tile * TOKEN_TILE + 1 + lax.broadcasted_iota(jnp.int32, (TOKEN_TILE, 1), 0)).astype(_F32)
    mixed = []
    win_sum, width = ext, 1
    for g, w in enumerate(POOL_WINDOWS):
        while width < w:
            win_sum = win_sum + pltpu.roll(win_sum, width, 0)
            width *= 2
        cols = slice(g * POOL_GROUP, (g + 1) * POOL_GROUP)
        pooled = win_sum[POOL_HALO:, cols] / jnp.minimum(t1, float(w)) - cur[:, cols]
        mixed.append(_dot(pooled.astype(_BF16), pw_ref[g]))
    mixed = (jnp.concatenate(mixed, axis=-1) * ps_ref[...]).astype(_BF16)
    merged = (sga_ref[...].astype(_F32) * _dot(attn_ref[...], wba_ref[...])
              + sgb_ref[...].astype(_F32) * _dot(mixed, wbp_ref[...]))
    o_ref[...] = x_ref[...] + _dot(merged.astype(_BF16), wo_ref[...])


def _merge(x, attn, pool, sga, sgb, wba, pw, ps, wbp, wo, seq):
    n = x.shape[0]
    tps = seq // TOKEN_TILE
    halo_blocks = TOKEN_TILE // POOL_HALO

    def flat(width):
        return pl.BlockSpec((TOKEN_TILE, width), lambda i: (i, 0))

    return pl.pallas_call(
        functools.partial(_merge_kernel, tiles_per_seq=tps),
        grid=(n // TOKEN_TILE,),
        in_specs=[
            flat(D_MODEL), flat(N_HEADS * HEAD_DIM), flat(POOL_WIDTH),
            pl.BlockSpec((POOL_HALO, POOL_WIDTH), lambda i: (jnp.maximum(i * halo_blocks - 1, 0), 0)),
            flat(D_MODEL), flat(D_MODEL),
            _const_spec((N_HEADS * HEAD_DIM, D_MODEL)), _const_spec((len(POOL_WINDOWS), POOL_GROUP, POOL_GROUP)),
            _const_spec((1, POOL_WIDTH)), _const_spec((POOL_WIDTH, D_MODEL)), _const_spec((D_MODEL, D_MODEL)),
        ],
        out_specs=flat(D_MODEL),
        out_shape=jax.ShapeDtypeStruct((n, D_MODEL), _F32),
        compiler_params=pltpu.CompilerParams(dimension_semantics=("arbitrary",), vmem_limit_bytes=VMEM_LIMIT_BYTES),
        name="merge_out",
    )(x, attn, pool, pool, sga, sgb, wba, pw, ps, wbp, wo)


def _split_w_in(w):
    cuts, off = {}, 0
    for name, width in (("q", N_HEADS * HEAD_DIM), ("ckv", KV_LATENT), ("qi", IDX_HEADS * IDX_DIM), ("ki", IDX_DIM),
                        ("wi", IDX_HEADS), ("pool", POOL_WIDTH), ("ga", D_MODEL), ("gb", D_MODEL)):
        cuts[name] = w[:, off:off + width]
        off += width
    pad_n = jnp.zeros((w.shape[0], LANES - IDX_DIM), w.dtype)
    wn = jnp.concatenate([cuts["ckv"], cuts["pool"], cuts["ga"], cuts["gb"], cuts["ki"], pad_n], axis=1)
    pad_t = jnp.zeros((w.shape[0], BF16_SUBLANES - IDX_HEADS), w.dtype)
    wt = jnp.concatenate([cuts["q"], cuts["qi"], cuts["wi"], pad_t], axis=1).T
    return wn.astype(_BF16), wt.astype(_BF16)


def kernel(x, norm_ffn1, ffn1_gate, ffn1_up, ffn1_down, norm_mix, w_in, norm_kv, w_uk, w_uv, pool_w, pool_scale,
           w_branch_attn, w_branch_pool, w_out, norm_ffn2, ffn2_gate, ffn2_up, ffn2_down, norm_final):
    batch, seq, _ = x.shape
    depth = norm_ffn1.shape[0]
    assert seq % TOKEN_TILE == 0 and seq % Q_TILE == 0 and batch % SEQ_PER_STEP == 0
    k_sel = min(TOPK_MAX, seq // 4)
    n = batch * seq
    bf = lambda a: a.astype(_BF16)
    row = lambda a: a.reshape(1, -1)
    gf = row(norm_final)
    h = x.reshape(n, D_MODEL)
    for i in range(depth):
        h = _ffn(h, row(norm_ffn1[i]), bf(ffn1_gate[i]), bf(ffn1_up[i]), bf(ffn1_down[i]), gf, False)
        wn, wt = _split_w_in(w_in[i])
        wuk_h = bf(jnp.transpose(w_uk[i], (1, 0, 2)))
        wuvt_h = bf(jnp.transpose(w_uv[i], (1, 2, 0)))
        qlt, ckv, ckvt, qit, ki, wit, pool, sga, sgb = _proj(h, row(norm_mix[i]), wn, wt, row(norm_kv[i]), wuk_h,
                                                             batch, seq)
        attn = _attn(qlt, qit, wit, ki.reshape(batch, seq, -1), ckv.reshape(batch, seq, -1), ckvt, wuvt_h, k_sel)
        h = _merge(h, attn.reshape(n, -1), pool, sga, sgb, bf(w_branch_attn[i]), bf(pool_w[i]), row(pool_scale[i]),
                   bf(w_branch_pool[i]), bf(w_out[i]), seq)
        h = _ffn(h, row(norm_ffn2[i]), bf(ffn2_gate[i]), bf(ffn2_up[i]), bf(ffn2_down[i]), gf, i == depth - 1)
    return h.reshape(batch, seq, D_MODEL)
```

```python
import functools

import jax
import jax.numpy as jnp
from jax import lax
from jax.experimental import pallas as pl
from jax.experimental.pallas import tpu as pltpu

D_MODEL = 1024
N_HEADS = 8
HEAD_DIM = 64
KV_LATENT = 128
ATTN_SCALE = HEAD_DIM ** -0.5
LOG2_E = 1.4426950408889634
IDX_HEADS = 8
IDX_DIM = 64
IDX_SCALE = (IDX_HEADS ** -0.5) * (IDX_DIM ** -0.5)
TOPK_MAX = 256
POOL_WINDOWS = (2, 4, 8, 16)
POOL_GROUP = 128
POOL_WIDTH = POOL_GROUP * len(POOL_WINDOWS)
POOL_HALO = 16
D_FF = 2816
EPS = 1e-6

LANES = 128
SUBLANES = 8
WORD_BITS = 32
BF16_SUBLANES = 16
KV_ROWS = KV_LATENT + BF16_SUBLANES
VMEM_LIMIT_BYTES = 56 * 1024 * 1024

TOKEN_TILE = 1024
MERGE_TILE = 512
FF_CHUNK = 256
Q_TILE = 256
K_CHUNK = SUBLANES * WORD_BITS
assert K_CHUNK == Q_TILE
ATTN_ROWS = 2 * K_CHUNK
SEQ_PER_STEP = 4

_N_CKV = 0
_N_POOL = _N_CKV + KV_LATENT
_N_GA = _N_POOL + POOL_WIDTH
_N_GB = _N_GA + D_MODEL
_N_KI = _N_GB + D_MODEL
_N_TOTAL = _N_KI + LANES
_T_Q = 0
_T_QI = _T_Q + N_HEADS * HEAD_DIM
_T_WI = _T_QI + IDX_HEADS * IDX_DIM
_T_TOTAL = _T_WI + BF16_SUBLANES

_F32 = jnp.float32
_BF16 = jnp.bfloat16
_INT_MIN = -(2 ** 31)
_NEG_BIG = float(jnp.finfo(jnp.float32).min)


def _const_spec(shape):
    return pl.BlockSpec(shape, lambda *_: (0,) * len(shape), pipeline_mode=pl.Buffered(1))


def _rms(x, g):
    return x * lax.rsqrt(jnp.mean(x * x, axis=-1, keepdims=True) + EPS) * g


def _dot(a, b):
    return jnp.dot(a, b, preferred_element_type=_F32)


def _ffn_body(x, g_ref, wg_ref, wu_ref, wd_ref, gf_ref, final_norm):
    h = _rms(x, g_ref[...]).astype(_BF16)
    acc = jnp.zeros(x.shape, _F32)
    for c in range(D_FF // FF_CHUNK):
        sl = slice(c * FF_CHUNK, (c + 1) * FF_CHUNK)
        gate = _dot(h, wg_ref[:, sl])
        up = _dot(h, wu_ref[:, sl])
        act = (gate * jax.nn.sigmoid(gate) * up).astype(_BF16)
        acc = acc + _dot(act, wd_ref[sl, :])
    y = x + 0.5 * acc
    if final_norm:
        y = _rms(y, gf_ref[...])
    return y


def _ffn_kernel(x_ref, g_ref, wg_ref, wu_ref, wd_ref, gf_ref, o_ref, *, final_norm):
    o_ref[...] = _ffn_body(x_ref[...], g_ref, wg_ref, wu_ref, wd_ref, gf_ref, final_norm)


def _ffn(x, g, wg, wu, wd, gf, final_norm):
    n = x.shape[0]
    tile = pl.BlockSpec((TOKEN_TILE, D_MODEL), lambda i: (i, 0))
    return pl.pallas_call(
        functools.partial(_ffn_kernel, final_norm=final_norm),
        grid=(n // TOKEN_TILE,),
        in_specs=[tile, _const_spec((1, D_MODEL)), _const_spec((D_MODEL, D_FF)), _const_spec((D_MODEL, D_FF)),
                  _const_spec((D_FF, D_MODEL)), _const_spec((1, D_MODEL))],
        out_specs=tile,
        out_shape=jax.ShapeDtypeStruct((n, D_MODEL), _F32),
        compiler_params=pltpu.CompilerParams(dimension_semantics=("arbitrary",), vmem_limit_bytes=VMEM_LIMIT_BYTES),
        name="ffn",
    )(x, g, wg, wu, wd, gf)


def _proj_kernel(x_ref, g_ref, wn_ref, wt_ref, gkv_ref, wuk_ref, qlt_ref, ckv_ref, ckvt_ref, qit_ref, ki_ref, wit_ref,
                 pool_ref, sga_ref, sgb_ref):
    h32 = _rms(x_ref[...], g_ref[...])
    h = h32.astype(_BF16)
    ht = h32.T.astype(_BF16)

    def z(lo, hi):
        return _dot(h, wn_ref[:, lo:hi])

    def zt(lo, hi):
        return _dot(wt_ref[lo:hi, :], ht)

    ckv = _rms(z(_N_CKV, _N_POOL), gkv_ref[...])
    ckv_ref[...] = ckv.astype(_BF16)
    ones_tile = (lax.broadcasted_iota(jnp.int32, (BF16_SUBLANES, TOKEN_TILE), 0) == 0).astype(_F32)
    ckvt_ref[0] = jnp.concatenate([ckv.T, ones_tile], axis=0).astype(_BF16)
    pool_ref[...] = z(_N_POOL, _N_GA)
    sga_ref[...] = jax.nn.sigmoid(z(_N_GA, _N_GB)).astype(_BF16)
    sgb_ref[...] = jax.nn.sigmoid(z(_N_GB, _N_KI)).astype(_BF16)
    ki_ref[...] = z(_N_KI, _N_TOTAL)[:, :IDX_DIM].astype(_BF16)

    qt = zt(_T_Q, _T_QI).astype(_BF16)
    for hd in range(N_HEADS):
        qlt = (_dot(wuk_ref[hd], qt[hd * HEAD_DIM:(hd + 1) * HEAD_DIM, :]) * (ATTN_SCALE * LOG2_E)).astype(_BF16)
        for j in range(TOKEN_TILE // Q_TILE):
            qlt_ref[0, j, :, hd * Q_TILE:(hd + 1) * Q_TILE] = qlt[:, j * Q_TILE:(j + 1) * Q_TILE]
    qit_ref[0] = zt(_T_QI, _T_WI).astype(_BF16)
    wit_ref[0] = zt(_T_WI, _T_TOTAL)[:IDX_HEADS, :] * IDX_SCALE


def _proj(x, g, wn, wt, gkv, wuk, batch, seq):
    n = x.shape[0]
    tps = seq // TOKEN_TILE

    def flat(width):
        return pl.BlockSpec((TOKEN_TILE, width), lambda i: (i, 0))

    def feat(rows):
        return pl.BlockSpec((1, rows, TOKEN_TILE), lambda i: (i // tps, 0, i % tps))

    out_shape = [
        jax.ShapeDtypeStruct((batch, seq // Q_TILE, KV_LATENT, N_HEADS * Q_TILE), _BF16),
        jax.ShapeDtypeStruct((n, KV_LATENT), _BF16),
        jax.ShapeDtypeStruct((batch, KV_ROWS, seq), _BF16),
        jax.ShapeDtypeStruct((batch, IDX_HEADS * IDX_DIM, seq), _BF16),
        jax.ShapeDtypeStruct((n, IDX_DIM), _BF16),
        jax.ShapeDtypeStruct((batch, IDX_HEADS, seq), _F32),
        jax.ShapeDtypeStruct((n, POOL_WIDTH), _F32),
        jax.ShapeDtypeStruct((n, D_MODEL), _BF16),
        jax.ShapeDtypeStruct((n, D_MODEL), _BF16),
    ]
    out_specs = [
        pl.BlockSpec((1, TOKEN_TILE // Q_TILE, KV_LATENT, N_HEADS * Q_TILE), lambda i: (i // tps, i % tps, 0, 0)),
        flat(KV_LATENT), feat(KV_ROWS), feat(IDX_HEADS * IDX_DIM), flat(IDX_DIM), feat(IDX_HEADS),
        flat(POOL_WIDTH), flat(D_MODEL), flat(D_MODEL),
    ]
    return pl.pallas_call(
        _proj_kernel,
        grid=(n // TOKEN_TILE,),
        in_specs=[flat(D_MODEL), _const_spec((1, D_MODEL)), _const_spec((D_MODEL, _N_TOTAL)),
                  _const_spec((_T_TOTAL, D_MODEL)), _const_spec((1, KV_LATENT)),
                  _const_spec((N_HEADS, KV_LATENT, HEAD_DIM))],
        out_specs=out_specs,
        out_shape=out_shape,
        compiler_params=pltpu.CompilerParams(dimension_semantics=("arbitrary",), vmem_limit_bytes=VMEM_LIMIT_BYTES),
        name="mix_proj",
    )(x, g, wn, wt, gkv, wuk)


def _bit_planes(u):
    a = [u[j * SUBLANES:(j + 1) * SUBLANES, :] for j in range(WORD_BITS)]
    j, m = WORD_BITS // 2, 0x0000FFFF
    while j:
        for k in range(WORD_BITS):
            if k & j == 0:
                t = (a[k] ^ (a[k + j] >> j)) & m
                a[k] = a[k] ^ t
                a[k + j] = a[k + j] ^ (t << j)
        j >>= 1
        m = m ^ (m << j)
    return a


def _popcount_rows(words):
    return jnp.sum(lax.population_count(words), axis=0, keepdims=True)


def _attn_kernel(qlt_ref, qit_ref, wit_ref, qit_next_ref, wit_next_ref, ki_ref, ckv_ref, ckvt_ref, wuvt_ref, o_ref,
                 planes_ref, sel_ref, eq_ref, ngt_ref, neq_ref, m_ref, acc_ref, *, k_sel):
    qb = pl.program_id(1)
    n_tiles = pl.num_programs(1)
    n_chunks = qb + 1
    slot = qb % 2
    q_pos = qb * Q_TILE + lax.broadcasted_iota(jnp.int32, (1, Q_TILE), 1)
    k_eff = jnp.minimum(k_sel, q_pos + 1)
    row_iota = lax.broadcasted_iota(jnp.int32, (K_CHUNK, Q_TILE), 0)
    lane_iota_rows = lax.broadcasted_iota(jnp.int32, (K_CHUNK, Q_TILE), 1)
    seqs = range(SEQ_PER_STEP)

    def chunk_start(c):
        return pl.multiple_of(c * K_CHUNK, K_CHUNK)

    def score_chunk(c, qit, wit, dst, diagonal):
        k0 = chunk_start(c)
        for s in seqs:
            kc = ki_ref[s, pl.ds(k0, K_CHUNK), :]
            score = jnp.zeros((K_CHUNK, Q_TILE), _F32)
            for j in range(IDX_HEADS):
                r = _dot(kc, qit[s, j * IDX_DIM:(j + 1) * IDX_DIM, :])
                score = score + jnp.maximum(r, 0.0) * wit[s, j:j + 1, :]
            bits = pltpu.bitcast(score, jnp.int32)
            key = bits ^ ((bits >> 31) & 0x7FFFFFFF)
            if diagonal:
                key = jnp.where(row_iota <= lane_iota_rows, key, _INT_MIN)
            for p, plane in enumerate(_bit_planes(key ^ _INT_MIN)):
                planes_ref[dst, s, c, p] = plane

    if K_CHUNK > k_sel:
        @pl.when(qb == 0)
        def _():
            score_chunk(0, qit_ref, wit_ref, slot, True)

    def search(nc):
        for s in seqs:
            for c in range(nc):
                eq_ref[s, c] = jnp.full((SUBLANES, Q_TILE), -1, jnp.int32)
                sel_ref[s, c] = jnp.zeros((SUBLANES, Q_TILE), jnp.int32)

        def decide_bit(p, n_gts):
            out = []
            for s in seqs:
                planes = [planes_ref[slot, s, c, pl.ds(p, 1)][0] for c in range(nc)]
                ones = [eq_ref[s, c] & planes[c] for c in range(nc)]
                n_ones = jnp.sum(functools.reduce(jnp.add, [lax.population_count(x) for x in ones]), axis=0,
                                 keepdims=True)
                take = n_gts[s] + n_ones >= k_eff
                out.append(jnp.where(take, n_gts[s], n_gts[s] + n_ones))
                flip = jnp.where(take, 0, -1)
                for c in range(nc):
                    sel_ref[s, c] = sel_ref[s, c] | (ones[c] & flip)
                    eq_ref[s, c] = eq_ref[s, c] & (planes[c] ^ flip)
            return tuple(out)

        n_gts = lax.fori_loop(0, WORD_BITS, decide_bit, (jnp.zeros((1, Q_TILE), jnp.int32),) * SEQ_PER_STEP)
        for s in seqs:
            ngt_ref[s, 0:1, :] = n_gts[s]
            neq_ref[s, 0:1, :] = functools.reduce(jnp.add, [_popcount_rows(eq_ref[s, c]) for c in range(nc)])
            for c in range(nc):
                sel_ref[s, c] = sel_ref[s, c] | eq_ref[s, c]

    sub_iota = lax.broadcasted_iota(jnp.int32, (SUBLANES, Q_TILE), 0)
    lane_iota = lax.broadcasted_iota(jnp.int32, (SUBLANES, Q_TILE), 1)
    n_groups = jnp.where(lane_iota >= sub_iota, ((lane_iota - sub_iota) >> (SUBLANES.bit_length() - 1)) + 1, 0)
    causal_words = jnp.where(n_groups > 0, jnp.left_shift(jnp.int32(-1), WORD_BITS - n_groups), 0)

    n_max = planes_ref.shape[2]
    for nc in range(1, n_max + 1):
        @pl.when(n_chunks == nc)
        def _(nc=nc):
            if nc * K_CHUNK <= k_sel:
                for s in seqs:
                    for c in range(nc):
                        sel_ref[s, c] = jnp.where(c < qb, -1, causal_words)
                    ngt_ref[s, 0:1, :] = k_eff
                    neq_ref[s, 0:1, :] = jnp.zeros((1, Q_TILE), jnp.int32)
            else:
                search(nc)

    need = [k_eff - ngt_ref[s, 0:1, :] for s in seqs]
    surplus = [neq_ref[s, 0:1, :] - need[s] for s in seqs]

    @pl.when(functools.reduce(jnp.maximum, [jnp.max(x) for x in surplus]) > 0)
    def _():
        sublane = lax.broadcasted_iota(jnp.int32, (SUBLANES, Q_TILE), 0)
        group_bits = WORD_BITS.bit_length() - 1
        row_bits = SUBLANES.bit_length() - 1

        def before_limit(lim):
            group = (lim >> row_bits) & (WORD_BITS - 1)
            this_bit = jnp.left_shift(jnp.int32(1), WORD_BITS - 1 - group)
            earlier_groups = ~(jnp.left_shift(jnp.int32(2), WORD_BITS - 1 - group) - 1)
            return earlier_groups | jnp.where(sublane < (lim & (SUBLANES - 1)), this_bit, 0)

        def chunk_mask(c, lim, within):
            c_lim = lim >> (row_bits + group_bits)
            return jnp.where(c < c_lim, -1, jnp.where(c == c_lim, within, 0))

        def ties_before(limits):
            within = [before_limit(lim) for lim in limits]

            def body(c, accs):
                return tuple(accs[s] + lax.population_count(eq_ref[s, c] & chunk_mask(c, limits[s], within[s]))
                             for s in seqs)

            zero = jnp.zeros((SUBLANES, Q_TILE), jnp.int32)
            accs = lax.fori_loop(0, n_chunks, body, (zero,) * SEQ_PER_STEP)
            return [jnp.sum(a, axis=0, keepdims=True) for a in accs]

        pos_bits = (planes_ref.shape[2] * K_CHUNK).bit_length()

        def grow(it, limits):
            cands = [lim + jnp.left_shift(jnp.int32(1), pos_bits - 1 - it) for lim in limits]
            n_before = ties_before(cands)
            return tuple(jnp.where(n_before[s] <= need[s], cands[s], limits[s]) for s in seqs)

        limits = lax.fori_loop(0, pos_bits, grow, (jnp.zeros((1, Q_TILE), jnp.int32),) * SEQ_PER_STEP)

        kept = [before_limit(lim) for lim in limits]

        def demote(c, _):
            for s in seqs:
                sel_ref[s, c] = sel_ref[s, c] & (~eq_ref[s, c] | chunk_mask(c, limits[s], kept[s]))
            return 0

        lax.fori_loop(0, n_chunks, demote, 0)

    m_ref[...] = jnp.full(m_ref.shape, _NEG_BIG, _F32)
    acc_ref[...] = jnp.zeros(acc_ref.shape, _F32)

    def select_bias(s, k0, size):
        rows = []
        for i in range(size // K_CHUNK):
            words = sel_ref[s, k0 // K_CHUNK + i]
            rows += [jnp.where(jnp.left_shift(words, j) < 0, 0.0, _NEG_BIG) for j in range(WORD_BITS)]
        return jnp.concatenate(rows, axis=0)

    def attend_rows(k0, size):
        for first in range(0, SEQ_PER_STEP, 2):
            pair = range(first, min(first + 2, SEQ_PER_STEP))
            logits, m_news = {}, {}
            for s in pair:
                bias = select_bias(s, k0, size)
                logits[s] = (_dot(ckv_ref[s, pl.ds(k0, size), :], qlt_ref[s, 0])
                             + jnp.concatenate([bias] * N_HEADS, axis=1))
                m_news[s] = jnp.maximum(m_ref[s], jnp.max(logits[s], axis=0, keepdims=True))
            for s in pair:
                p = jnp.exp2(logits[s] - m_news[s]).astype(_BF16)
                acc_ref[s] = jnp.exp2(m_ref[s] - m_news[s]) * acc_ref[s] + _dot(ckvt_ref[s, :, pl.ds(k0, size)], p)
                m_ref[s] = m_news[s]

    per_wide = ATTN_ROWS // K_CHUNK

    def attend_all(score_next):
        def wide(b, _):
            attend_rows(pl.multiple_of(b * ATTN_ROWS, ATTN_ROWS), ATTN_ROWS)
            if score_next:
                for r in range(per_wide):
                    score_chunk(b * per_wide + r, qit_next_ref, wit_next_ref, 1 - slot, False)
            return 0

        lax.fori_loop(0, n_chunks // per_wide, wide, 0)
        for r in range(1, per_wide):
            @pl.when(n_chunks % per_wide >= r)
            def _(r=r):
                c = n_chunks - n_chunks % per_wide + r - 1
                attend_rows(chunk_start(c), K_CHUNK)
                if score_next:
                    score_chunk(c, qit_next_ref, wit_next_ref, 1 - slot, False)
        if score_next:
            score_chunk(qb + 1, qit_next_ref, wit_next_ref, 1 - slot, True)

    @pl.when(qb + 1 < n_tiles)
    def _():
        attend_all(True)

    @pl.when(qb + 1 == n_tiles)
    def _():
        attend_all(False)

    for s in seqs:
        o_lat_t = (acc_ref[s, :KV_LATENT, :] / acc_ref[s, KV_LATENT:KV_LATENT + 1, :]).astype(_BF16)
        out_t = [_dot(wuvt_ref[hd], o_lat_t[:, hd * Q_TILE:(hd + 1) * Q_TILE]) for hd in range(N_HEADS)]
        o_ref[s] = jnp.concatenate(out_t, axis=0).T.astype(_BF16)


def _attn(qlt, qit, wit, ki, ckv, ckvt, wuvt, k_sel):
    batch, _, seq = ckvt.shape
    sp = SEQ_PER_STEP
    last_tile = seq // Q_TILE - 1
    return pl.pallas_call(
        functools.partial(_attn_kernel, k_sel=k_sel),
        grid=(batch // sp, seq // Q_TILE),
        in_specs=[
            pl.BlockSpec((sp, 1, KV_LATENT, N_HEADS * Q_TILE), lambda b, i: (b, i, 0, 0)),
            pl.BlockSpec((sp, IDX_HEADS * IDX_DIM, Q_TILE), lambda b, i: (b, 0, i)),
            pl.BlockSpec((sp, IDX_HEADS, Q_TILE), lambda b, i: (b, 0, i)),
            pl.BlockSpec((sp, IDX_HEADS * IDX_DIM, Q_TILE), lambda b, i: (b, 0, jnp.minimum(i + 1, last_tile))),
            pl.BlockSpec((sp, IDX_HEADS, Q_TILE), lambda b, i: (b, 0, jnp.minimum(i + 1, last_tile))),
            pl.BlockSpec((sp, seq, IDX_DIM), lambda b, i: (b, 0, 0)),
            pl.BlockSpec((sp, seq, KV_LATENT), lambda b, i: (b, 0, 0)),
            pl.BlockSpec((sp, KV_ROWS, seq), lambda b, i: (b, 0, 0)),
            _const_spec((N_HEADS, HEAD_DIM, KV_LATENT)),
        ],
        out_specs=pl.BlockSpec((sp, Q_TILE, N_HEADS * HEAD_DIM), lambda b, i: (b, i, 0)),
        out_shape=jax.ShapeDtypeStruct((batch, seq, N_HEADS * HEAD_DIM), _BF16),
        scratch_shapes=[pltpu.VMEM((2, sp, seq // K_CHUNK, WORD_BITS, SUBLANES, Q_TILE), jnp.int32),
                        pltpu.VMEM((sp, seq // K_CHUNK, SUBLANES, Q_TILE), jnp.int32),
                        pltpu.VMEM((sp, seq // K_CHUNK, SUBLANES, Q_TILE), jnp.int32),
                        pltpu.VMEM((sp, SUBLANES, Q_TILE), jnp.int32), pltpu.VMEM((sp, SUBLANES, Q_TILE), jnp.int32),
                        pltpu.VMEM((sp, 1, N_HEADS * Q_TILE), _F32),
                        pltpu.VMEM((sp, KV_ROWS, N_HEADS * Q_TILE), _F32)],
        compiler_params=pltpu.CompilerParams(dimension_semantics=("arbitrary", "arbitrary"),
                                             vmem_limit_bytes=VMEM_LIMIT_BYTES),
        name="dsa_attn",
    )(qlt, qit, wit, qit, wit, ki, ckv, ckvt, wuvt)


def _merge_ffn_kernel(x_ref, attn_ref, pool_ref, halo_ref, sga_ref, sgb_ref, wba_ref, pw_ref, ps_ref, wbp_ref, wo_ref,
                      g_ref, wg_ref, wu_ref, wd_ref, gf_ref, o_ref, *, tiles_per_seq, final_norm):
    tile = pl.program_id(0) % tiles_per_seq
    cur = pool_ref[...]
    halo = jnp.where(tile == 0, 0.0, halo_ref[...])
    ext = jnp.concatenate([halo, cur], axis=0)
    t1 = (tile * MERGE_TILE + 1 + lax.broadcasted_iota(jnp.int32, (MERGE_TILE, 1), 0)).astype(_F32)
    mixed = []
    win_sum, width = ext, 1
    for g, w in enumerate(POOL_WINDOWS):
        while width < w:
            win_sum = win_sum + pltpu.roll(win_sum, width, 0)
            width *= 2
        cols = slice(g * POOL_GROUP, (g + 1) * POOL_GROUP)
        pooled = win_sum[POOL_HALO:, cols] / jnp.minimum(t1, float(w)) - cur[:, cols]
        mixed.append(_dot(pooled.astype(_BF16), pw_ref[g]))
    mixed = (jnp.concatenate(mixed, axis=-1) * ps_ref[...]).astype(_BF16)
    merged = (sga_ref[...].astype(_F32) * _dot(attn_ref[...], wba_ref[...])
              + sgb_ref[...].astype(_F32) * _dot(mixed, wbp_ref[...]))
    x = x_ref[...] + _dot(merged.astype(_BF16), wo_ref[...])
    o_ref[...] = _ffn_body(x, g_ref, wg_ref, wu_ref, wd_ref, gf_ref, final_norm)


def _merge_ffn(x, attn, pool, sga, sgb, wba, pw, ps, wbp, wo, g, wg, wu, wd, gf, final_norm, seq):
    n = x.shape[0]
    tps = seq // MERGE_TILE
    halo_blocks = MERGE_TILE // POOL_HALO

    def flat(width):
        return pl.BlockSpec((MERGE_TILE, width), lambda i: (i, 0))

    return pl.pallas_call(
        functools.partial(_merge_ffn_kernel, tiles_per_seq=tps, final_norm=final_norm),
        grid=(n // MERGE_TILE,),
        in_specs=[
            flat(D_MODEL), flat(N_HEADS * HEAD_DIM), flat(POOL_WIDTH),
            pl.BlockSpec((POOL_HALO, POOL_WIDTH), lambda i: (jnp.maximum(i * halo_blocks - 1, 0), 0)),
            flat(D_MODEL), flat(D_MODEL),
            _const_spec((N_HEADS * HEAD_DIM, D_MODEL)), _const_spec((len(POOL_WINDOWS), POOL_GROUP, POOL_GROUP)),
            _const_spec((1, POOL_WIDTH)), _const_spec((POOL_WIDTH, D_MODEL)), _const_spec((D_MODEL, D_MODEL)),
            _const_spec((1, D_MODEL)), _const_spec((D_MODEL, D_FF)), _const_spec((D_MODEL, D_FF)),
            _const_spec((D_FF, D_MODEL)), _const_spec((1, D_MODEL)),
        ],
        out_specs=flat(D_MODEL),
        out_shape=jax.ShapeDtypeStruct((n, D_MODEL), _F32),
        compiler_params=pltpu.CompilerParams(dimension_semantics=("arbitrary",), vmem_limit_bytes=VMEM_LIMIT_BYTES),
        name="merge_ffn",
    )(x, attn, pool, pool, sga, sgb, wba, pw, ps, wbp, wo, g, wg, wu, wd, gf)


def _split_w_in(w):
    cuts, off = {}, 0
    for name, width in (("q", N_HEADS * HEAD_DIM), ("ckv", KV_LATENT), ("qi", IDX_HEADS * IDX_DIM), ("ki", IDX_DIM),
                        ("wi", IDX_HEADS), ("pool", POOL_WIDTH), ("ga", D_MODEL), ("gb", D_MODEL)):
        cuts[name] = w[:, off:off + width]
        off += width
    pad_n = jnp.zeros((w.shape[0], LANES - IDX_DIM), w.dtype)
    wn = jnp.concatenate([cuts["ckv"], cuts["pool"], cuts["ga"], cuts["gb"], cuts["ki"], pad_n], axis=1)
    pad_t = jnp.zeros((w.shape[0], BF16_SUBLANES - IDX_HEADS), w.dtype)
    wt = jnp.concatenate([cuts["q"], cuts["qi"], cuts["wi"], pad_t], axis=1).T
    return wn.astype(_BF16), wt.astype(_BF16)


def kernel(x, norm_ffn1, ffn1_gate, ffn1_up, ffn1_down, norm_mix, w_in, norm_kv, w_uk, w_uv, pool_w, pool_scale,
           w_branch_attn, w_branch_pool, w_out, norm_ffn2, ffn2_gate, ffn2_up, ffn2_down, norm_final):
    batch, seq, _ = x.shape
    depth = norm_ffn1.shape[0]
    assert seq % TOKEN_TILE == 0 and seq % MERGE_TILE == 0 and seq % Q_TILE == 0 and batch % SEQ_PER_STEP == 0
    k_sel = min(TOPK_MAX, seq // 4)
    n = batch * seq
    bf = lambda a: a.astype(_BF16)
    row = lambda a: a.reshape(1, -1)
    gf = row(norm_final)
    h = x.reshape(n, D_MODEL)
    for i in range(depth):
        h = _ffn(h, row(norm_ffn1[i]), bf(ffn1_gate[i]), bf(ffn1_up[i]), bf(ffn1_down[i]), gf, False)
        wn, wt = _split_w_in(w_in[i])
        wuk_h = bf(jnp.transpose(w_uk[i], (1, 0, 2)))
        wuvt_h = bf(jnp.transpose(w_uv[i], (1, 2, 0)))
        qlt, ckv, ckvt, qit, ki, wit, pool, sga, sgb = _proj(h, row(norm_mix[i]), wn, wt, row(norm_kv[i]), wuk_h,
                                                             batch, seq)
        attn = _attn(qlt, qit, wit, ki.reshape(batch, seq, -1), ckv.reshape(batch, seq, -1), ckvt, wuvt_h, k_sel)
        h = _merge_ffn(h, attn.reshape(n, -1), pool, sga, sgb, bf(w_branch_attn[i]), bf(pool_w[i]),
                       row(pool_scale[i]), bf(w_branch_pool[i]), bf(w_out[i]), row(norm_ffn2[i]), bf(ffn2_gate[i]),
                       bf(ffn2_up[i]), bf(ffn2_down[i]), gf, i == depth - 1, seq)
    return h.reshape(batch, seq, D_MODEL)
```

```python
import functools

import jax
import jax.numpy as jnp
from jax import lax
from jax.experimental import pallas as pl
from jax.experimental.pallas import tpu as pltpu

D_MODEL = 1024
N_HEADS = 8
HEAD_DIM = 64
KV_LATENT = 128
ATTN_SCALE = HEAD_DIM ** -0.5
LOG2_E = 1.4426950408889634
IDX_HEADS = 8
IDX_DIM = 64
IDX_SCALE = (IDX_HEADS ** -0.5) * (IDX_DIM ** -0.5)
TOPK_MAX = 256
POOL_WINDOWS = (2, 4, 8, 16)
POOL_GROUP = 128
POOL_WIDTH = POOL_GROUP * len(POOL_WINDOWS)
POOL_HALO = 16
D_FF = 2816
EPS = 1e-6

LANES = 128
SUBLANES = 8
WORD_BITS = 32
BF16_SUBLANES = 16
KV_ROWS = KV_LATENT + BF16_SUBLANES
VMEM_LIMIT_BYTES = 56 * 1024 * 1024

TOKEN_TILE = 1024
MERGE_TILE = 512
FF_CHUNK = 256
Q_TILE = 256
K_CHUNK = SUBLANES * WORD_BITS
assert K_CHUNK == Q_TILE
ATTN_ROWS = 2 * K_CHUNK
SEQ_PER_STEP = 4

_N_CKV = 0
_N_POOL = _N_CKV + KV_LATENT
_N_GA = _N_POOL + POOL_WIDTH
_N_GB = _N_GA + D_MODEL
_N_KI = _N_GB + D_MODEL
_N_TOTAL = _N_KI + LANES
_T_Q = 0
_T_QI = _T_Q + N_HEADS * HEAD_DIM
_T_WI = _T_QI + IDX_HEADS * IDX_DIM
_T_TOTAL = _T_WI + BF16_SUBLANES

_F32 = jnp.float32
_BF16 = jnp.bfloat16
_INT_MIN = -(2 ** 31)
_NEG_BIG = float(jnp.finfo(jnp.float32).min)


def _const_spec(shape):
    return pl.BlockSpec(shape, lambda *_: (0,) * len(shape), pipeline_mode=pl.Buffered(1))


def _rms(x, g):
    return x * lax.rsqrt(jnp.mean(x * x, axis=-1, keepdims=True) + EPS) * g


def _dot(a, b):
    return jnp.dot(a, b, preferred_element_type=_F32)


def _ffn_body(x, g_ref, wg_ref, wu_ref, wd_ref, gf_ref, final_norm):
    h = _rms(x, g_ref[...]).astype(_BF16)
    acts = []
    for c in range(D_FF // FF_CHUNK):
        sl = slice(c * FF_CHUNK, (c + 1) * FF_CHUNK)
        gate = _dot(h, wg_ref[:, sl])
        up = _dot(h, wu_ref[:, sl])
        acts.append((gate * jax.nn.sigmoid(gate) * up).astype(_BF16))
    y = x + 0.5 * _dot(jnp.concatenate(acts, axis=1), wd_ref[...])
    if final_norm:
        y = _rms(y, gf_ref[...])
    return y


def _ffn_kernel(x_ref, g_ref, wg_ref, wu_ref, wd_ref, gf_ref, o_ref, *, final_norm):
    o_ref[...] = _ffn_body(x_ref[...], g_ref, wg_ref, wu_ref, wd_ref, gf_ref, final_norm)


def _ffn(x, g, wg, wu, wd, gf, final_norm):
    n = x.shape[0]
    tile = pl.BlockSpec((TOKEN_TILE, D_MODEL), lambda i: (i, 0))
    return pl.pallas_call(
        functools.partial(_ffn_kernel, final_norm=final_norm),
        grid=(n // TOKEN_TILE,),
        in_specs=[tile, _const_spec((1, D_MODEL)), _const_spec((D_MODEL, D_FF)), _const_spec((D_MODEL, D_FF)),
                  _const_spec((D_FF, D_MODEL)), _const_spec((1, D_MODEL))],
        out_specs=tile,
        out_shape=jax.ShapeDtypeStruct((n, D_MODEL), _F32),
        compiler_params=pltpu.CompilerParams(dimension_semantics=("arbitrary",), vmem_limit_bytes=VMEM_LIMIT_BYTES),
        name="ffn",
    )(x, g, wg, wu, wd, gf)


def _proj_kernel(x_ref, g_ref, wn_ref, wt_ref, gkv_ref, wuk_ref, qlt_ref, ckv_ref, ckvt_ref, qit_ref, ki_ref, wit_ref,
                 pool_ref, sga_ref, sgb_ref):
    h32 = _rms(x_ref[...], g_ref[...])
    h = h32.astype(_BF16)
    ht = h32.T.astype(_BF16)

    def z(lo, hi):
        return _dot(h, wn_ref[:, lo:hi])

    def zt(lo, hi):
        return _dot(wt_ref[lo:hi, :], ht)

    ckv = _rms(z(_N_CKV, _N_POOL), gkv_ref[...])
    ckv_ref[...] = ckv.astype(_BF16)
    ones_tile = (lax.broadcasted_iota(jnp.int32, (BF16_SUBLANES, TOKEN_TILE), 0) == 0).astype(_F32)
    ckvt_ref[0] = jnp.concatenate([ckv.T, ones_tile], axis=0).astype(_BF16)
    pool_ref[...] = z(_N_POOL, _N_GA)
    sga_ref[...] = jax.nn.sigmoid(z(_N_GA, _N_GB)).astype(_BF16)
    sgb_ref[...] = jax.nn.sigmoid(z(_N_GB, _N_KI)).astype(_BF16)
    ki_ref[...] = z(_N_KI, _N_TOTAL)[:, :IDX_DIM].astype(_BF16)

    qt = zt(_T_Q, _T_QI).astype(_BF16)
    for hd in range(N_HEADS):
        qlt = (_dot(wuk_ref[hd], qt[hd * HEAD_DIM:(hd + 1) * HEAD_DIM, :]) * (ATTN_SCALE * LOG2_E)).astype(_BF16)
        for j in range(TOKEN_TILE // Q_TILE):
            qlt_ref[0, j, :, hd * Q_TILE:(hd + 1) * Q_TILE] = qlt[:, j * Q_TILE:(j + 1) * Q_TILE]
    qit_ref[0] = zt(_T_QI, _T_WI).astype(_BF16)
    wit_ref[0] = zt(_T_WI, _T_TOTAL)[:IDX_HEADS, :] * IDX_SCALE


def _proj(x, g, wn, wt, gkv, wuk, batch, seq):
    n = x.shape[0]
    tps = seq // TOKEN_TILE

    def flat(width):
        return pl.BlockSpec((TOKEN_TILE, width), lambda i: (i, 0))

    def feat(rows):
        return pl.BlockSpec((1, rows, TOKEN_TILE), lambda i: (i // tps, 0, i % tps))

    out_shape = [
        jax.ShapeDtypeStruct((batch, seq // Q_TILE, KV_LATENT, N_HEADS * Q_TILE), _BF16),
        jax.ShapeDtypeStruct((n, KV_LATENT), _BF16),
        jax.ShapeDtypeStruct((batch, KV_ROWS, seq), _BF16),
        jax.ShapeDtypeStruct((batch, IDX_HEADS * IDX_DIM, seq), _BF16),
        jax.ShapeDtypeStruct((n, IDX_DIM), _BF16),
        jax.ShapeDtypeStruct((batch, IDX_HEADS, seq), _F32),
        jax.ShapeDtypeStruct((n, POOL_WIDTH), _F32),
        jax.ShapeDtypeStruct((n, D_MODEL), _BF16),
        jax.ShapeDtypeStruct((n, D_MODEL), _BF16),
    ]
    out_specs = [
        pl.BlockSpec((1, TOKEN_TILE // Q_TILE, KV_LATENT, N_HEADS * Q_TILE), lambda i: (i // tps, i % tps, 0, 0)),
        flat(KV_LATENT), feat(KV_ROWS), feat(IDX_HEADS * IDX_DIM), flat(IDX_DIM), feat(IDX_HEADS),
        flat(POOL_WIDTH), flat(D_MODEL), flat(D_MODEL),
    ]
    return pl.pallas_call(
        _proj_kernel,
        grid=(n // TOKEN_TILE,),
        in_specs=[flat(D_MODEL), _const_spec((1, D_MODEL)), _const_spec((D_MODEL, _N_TOTAL)),
                  _const_spec((_T_TOTAL, D_MODEL)), _const_spec((1, KV_LATENT)),
                  _const_spec((N_HEADS, KV_LATENT, HEAD_DIM))],
        out_specs=out_specs,
        out_shape=out_shape,
        compiler_params=pltpu.CompilerParams(dimension_semantics=("arbitrary",), vmem_limit_bytes=VMEM_LIMIT_BYTES),
        name="mix_proj",
    )(x, g, wn, wt, gkv, wuk)


def _bit_planes(u):
    a = [u[j * SUBLANES:(j + 1) * SUBLANES, :] for j in range(WORD_BITS)]
    j, m = WORD_BITS // 2, 0x0000FFFF
    while j:
        for k in range(WORD_BITS):
            if k & j == 0:
                t = (a[k] ^ (a[k + j] >> j)) & m
                a[k] = a[k] ^ t
                a[k + j] = a[k + j] ^ (t << j)
        j >>= 1
        m = m ^ (m << j)
    return a


def _popcount_rows(words):
    return jnp.sum(lax.population_count(words), axis=0, keepdims=True)


def _attn_kernel(qlt_ref, qit_ref, wit_ref, qit_next_ref, wit_next_ref, ki_ref, ckv_ref, ckvt_ref, wuvt_ref, o_ref,
                 planes_ref, sel_ref, eq_ref, ngt_ref, neq_ref, m_ref, acc_ref, *, k_sel):
    qb = pl.program_id(1)
    n_tiles = pl.num_programs(1)
    n_chunks = qb + 1
    slot = qb % 2
    q_pos = qb * Q_TILE + lax.broadcasted_iota(jnp.int32, (1, Q_TILE), 1)
    k_eff = jnp.minimum(k_sel, q_pos + 1)
    row_iota = lax.broadcasted_iota(jnp.int32, (K_CHUNK, Q_TILE), 0)
    lane_iota_rows = lax.broadcasted_iota(jnp.int32, (K_CHUNK, Q_TILE), 1)
    seqs = range(SEQ_PER_STEP)

    def chunk_start(c):
        return pl.multiple_of(c * K_CHUNK, K_CHUNK)

    def score_chunk(c, qit, wit, dst, diagonal):
        k0 = chunk_start(c)
        for s in seqs:
            kc = ki_ref[s, pl.ds(k0, K_CHUNK), :]
            score = jnp.zeros((K_CHUNK, Q_TILE), _F32)
            for j in range(IDX_HEADS):
                r = _dot(kc, qit[s, j * IDX_DIM:(j + 1) * IDX_DIM, :])
                score = score + jnp.maximum(r, 0.0) * wit[s, j:j + 1, :]
            bits = pltpu.bitcast(score, jnp.int32)
            key = bits ^ ((bits >> 31) & 0x7FFFFFFF)
            if diagonal:
                key = jnp.where(row_iota <= lane_iota_rows, key, _INT_MIN)
            for p, plane in enumerate(_bit_planes(key ^ _INT_MIN)):
                planes_ref[dst, s, c, p] = plane

    if K_CHUNK > k_sel:
        @pl.when(qb == 0)
        def _():
            score_chunk(0, qit_ref, wit_ref, slot, True)

    def search(nc):
        for s in seqs:
            for c in range(nc):
                eq_ref[s, c] = jnp.full((SUBLANES, Q_TILE), -1, jnp.int32)
                sel_ref[s, c] = jnp.zeros((SUBLANES, Q_TILE), jnp.int32)

        def decide_bit(p, n_gts):
            out = []
            for s in seqs:
                planes = [planes_ref[slot, s, c, pl.ds(p, 1)][0] for c in range(nc)]
                ones = [eq_ref[s, c] & planes[c] for c in range(nc)]
                n_ones = jnp.sum(functools.reduce(jnp.add, [lax.population_count(x) for x in ones]), axis=0,
                                 keepdims=True)
                take = n_gts[s] + n_ones >= k_eff
                out.append(jnp.where(take, n_gts[s], n_gts[s] + n_ones))
                flip = jnp.where(take, 0, -1)
                for c in range(nc):
                    sel_ref[s, c] = sel_ref[s, c] | (ones[c] & flip)
                    eq_ref[s, c] = eq_ref[s, c] & (planes[c] ^ flip)
            return tuple(out)

        n_gts = lax.fori_loop(0, WORD_BITS, decide_bit, (jnp.zeros((1, Q_TILE), jnp.int32),) * SEQ_PER_STEP)
        for s in seqs:
            ngt_ref[s, 0:1, :] = n_gts[s]
            neq_ref[s, 0:1, :] = functools.reduce(jnp.add, [_popcount_rows(eq_ref[s, c]) for c in range(nc)])
            for c in range(nc):
                sel_ref[s, c] = sel_ref[s, c] | eq_ref[s, c]

    sub_iota = lax.broadcasted_iota(jnp.int32, (SUBLANES, Q_TILE), 0)
    lane_iota = lax.broadcasted_iota(jnp.int32, (SUBLANES, Q_TILE), 1)
    n_groups = jnp.where(lane_iota >= sub_iota, ((lane_iota - sub_iota) >> (SUBLANES.bit_length() - 1)) + 1, 0)
    causal_words = jnp.where(n_groups > 0, jnp.left_shift(jnp.int32(-1), WORD_BITS - n_groups), 0)

    n_max = planes_ref.shape[2]
    for nc in range(1, n_max + 1):
        @pl.when(n_chunks == nc)
        def _(nc=nc):
            if nc * K_CHUNK <= k_sel:
                for s in seqs:
                    for c in range(nc):
                        sel_ref[s, c] = jnp.where(c < qb, -1, causal_words)
                    ngt_ref[s, 0:1, :] = k_eff
                    neq_ref[s, 0:1, :] = jnp.zeros((1, Q_TILE), jnp.int32)
            else:
                search(nc)

    need = [k_eff - ngt_ref[s, 0:1, :] for s in seqs]
    surplus = [neq_ref[s, 0:1, :] - need[s] for s in seqs]

    @pl.when(functools.reduce(jnp.maximum, [jnp.max(x) for x in surplus]) > 0)
    def _():
        sublane = lax.broadcasted_iota(jnp.int32, (SUBLANES, Q_TILE), 0)
        group_bits = WORD_BITS.bit_length() - 1
        row_bits = SUBLANES.bit_length() - 1

        def before_limit(lim):
            group = (lim >> row_bits) & (WORD_BITS - 1)
            this_bit = jnp.left_shift(jnp.int32(1), WORD_BITS - 1 - group)
            earlier_groups = ~(jnp.left_shift(jnp.int32(2), WORD_BITS - 1 - group) - 1)
            return earlier_groups | jnp.where(sublane < (lim & (SUBLANES - 1)), this_bit, 0)

        def chunk_mask(c, lim, within):
            c_lim = lim >> (row_bits + group_bits)
            return jnp.where(c < c_lim, -1, jnp.where(c == c_lim, within, 0))

        def ties_before(limits):
            within = [before_limit(lim) for lim in limits]

            def body(c, accs):
                return tuple(accs[s] + lax.population_count(eq_ref[s, c] & chunk_mask(c, limits[s], within[s]))
                             for s in seqs)

            zero = jnp.zeros((SUBLANES, Q_TILE), jnp.int32)
            accs = lax.fori_loop(0, n_chunks, body, (zero,) * SEQ_PER_STEP)
            return [jnp.sum(a, axis=0, keepdims=True) for a in accs]

        pos_bits = (planes_ref.shape[2] * K_CHUNK).bit_length()

        def grow(it, limits):
            cands = [lim + jnp.left_shift(jnp.int32(1), pos_bits - 1 - it) for lim in limits]
            n_before = ties_before(cands)
            return tuple(jnp.where(n_before[s] <= need[s], cands[s], limits[s]) for s in seqs)

        limits = lax.fori_loop(0, pos_bits, grow, (jnp.zeros((1, Q_TILE), jnp.int32),) * SEQ_PER_STEP)

        kept = [before_limit(lim) for lim in limits]

        def demote(c, _):
            for s in seqs:
                sel_ref[s, c] = sel_ref[s, c] & (~eq_ref[s, c] | chunk_mask(c, limits[s], kept[s]))
            return 0

        lax.fori_loop(0, n_chunks, demote, 0)

    m_ref[...] = jnp.full(m_ref.shape, _NEG_BIG, _F32)
    acc_ref[...] = jnp.zeros(acc_ref.shape, _F32)

    def select_bias(s, k0, size):
        rows = []
        for i in range(size // K_CHUNK):
            words = sel_ref[s, k0 // K_CHUNK + i]
            rows += [jnp.where(jnp.left_shift(words, j) < 0, 0.0, _NEG_BIG) for j in range(WORD_BITS)]
        return jnp.concatenate(rows, axis=0)

    def attend_rows(k0, size):
        for first in range(0, SEQ_PER_STEP, 2):
            pair = range(first, min(first + 2, SEQ_PER_STEP))
            logits, m_news = {}, {}
            for s in pair:
                bias = select_bias(s, k0, size)
                logits[s] = (_dot(ckv_ref[s, pl.ds(k0, size), :], qlt_ref[s, 0])
                             + jnp.concatenate([bias] * N_HEADS, axis=1))
                m_news[s] = jnp.maximum(m_ref[s], jnp.max(logits[s], axis=0, keepdims=True))
            for s in pair:
                p = jnp.exp2(logits[s] - m_news[s]).astype(_BF16)
                acc_ref[s] = jnp.exp2(m_ref[s] - m_news[s]) * acc_ref[s] + _dot(ckvt_ref[s, :, pl.ds(k0, size)], p)
                m_ref[s] = m_news[s]

    per_wide = ATTN_ROWS // K_CHUNK

    def attend_all(score_next):
        def wide(b, _):
            attend_rows(pl.multiple_of(b * ATTN_ROWS, ATTN_ROWS), ATTN_ROWS)
            if score_next:
                for r in range(per_wide):
                    score_chunk(b * per_wide + r, qit_next_ref, wit_next_ref, 1 - slot, False)
            return 0

        lax.fori_loop(0, n_chunks // per_wide, wide, 0)
        for r in range(1, per_wide):
            @pl.when(n_chunks % per_wide >= r)
            def _(r=r):
                c = n_chunks - n_chunks % per_wide + r - 1
                attend_rows(chunk_start(c), K_CHUNK)
                if score_next:
                    score_chunk(c, qit_next_ref, wit_next_ref, 1 - slot, False)
        if score_next:
            score_chunk(qb + 1, qit_next_ref, wit_next_ref, 1 - slot, True)

    @pl.when(qb + 1 < n_tiles)
    def _():
        attend_all(True)

    @pl.when(qb + 1 == n_tiles)
    def _():
        attend_all(False)

    for s in seqs:
        o_lat_t = (acc_ref[s, :KV_LATENT, :] / acc_ref[s, KV_LATENT:KV_LATENT + 1, :]).astype(_BF16)
        out_t = [_dot(wuvt_ref[hd], o_lat_t[:, hd * Q_TILE:(hd + 1) * Q_TILE]) for hd in range(N_HEADS)]
        o_ref[s] = jnp.concatenate(out_t, axis=0).T.astype(_BF16)


def _attn(qlt, qit, wit, ki, ckv, ckvt, wuvt, k_sel):
    batch, _, seq = ckvt.shape
    sp = SEQ_PER_STEP
    last_tile = seq // Q_TILE - 1
    return pl.pallas_call(
        functools.partial(_attn_kernel, k_sel=k_sel),
        grid=(batch // sp, seq // Q_TILE),
        in_specs=[
            pl.BlockSpec((sp, 1, KV_LATENT, N_HEADS * Q_TILE), lambda b, i: (b, i, 0, 0)),
            pl.BlockSpec((sp, IDX_HEADS * IDX_DIM, Q_TILE), lambda b, i: (b, 0, i)),
            pl.BlockSpec((sp, IDX_HEADS, Q_TILE), lambda b, i: (b, 0, i)),
            pl.BlockSpec((sp, IDX_HEADS * IDX_DIM, Q_TILE), lambda b, i: (b, 0, jnp.minimum(i + 1, last_tile))),
            pl.BlockSpec((sp, IDX_HEADS, Q_TILE), lambda b, i: (b, 0, jnp.minimum(i + 1, last_tile))),
            pl.BlockSpec((sp, seq, IDX_DIM), lambda b, i: (b, 0, 0)),
            pl.BlockSpec((sp, seq, KV_LATENT), lambda b, i: (b, 0, 0)),
            pl.BlockSpec((sp, KV_ROWS, seq), lambda b, i: (b, 0, 0)),
            _const_spec((N_HEADS, HEAD_DIM, KV_LATENT)),
        ],
        out_specs=pl.BlockSpec((sp, Q_TILE, N_HEADS * HEAD_DIM), lambda b, i: (b, i, 0)),
        out_shape=jax.ShapeDtypeStruct((batch, seq, N_HEADS * HEAD_DIM), _BF16),
        scratch_shapes=[pltpu.VMEM((2, sp, seq // K_CHUNK, WORD_BITS, SUBLANES, Q_TILE), jnp.int32),
                        pltpu.VMEM((sp, seq // K_CHUNK, SUBLANES, Q_TILE), jnp.int32),
                        pltpu.VMEM((sp, seq // K_CHUNK, SUBLANES, Q_TILE), jnp.int32),
                        pltpu.VMEM((sp, SUBLANES, Q_TILE), jnp.int32), pltpu.VMEM((sp, SUBLANES, Q_TILE), jnp.int32),
                        pltpu.VMEM((sp, 1, N_HEADS * Q_TILE), _F32),
                        pltpu.VMEM((sp, KV_ROWS, N_HEADS * Q_TILE), _F32)],
        compiler_params=pltpu.CompilerParams(dimension_semantics=("arbitrary", "arbitrary"),
                                             vmem_limit_bytes=VMEM_LIMIT_BYTES),
        name="dsa_attn",
    )(qlt, qit, wit, qit, wit, ki, ckv, ckvt, wuvt)


def _merge_ffn_kernel(x_ref, attn_ref, pool_ref, halo_ref, sga_ref, sgb_ref, wba_ref, pw_ref, ps_ref, wbp_ref, wo_ref,
                      g_ref, wg_ref, wu_ref, wd_ref, gf_ref, o_ref, *, tiles_per_seq, final_norm):
    tile = pl.program_id(0) % tiles_per_seq
    cur = pool_ref[...]
    halo = jnp.where(tile == 0, 0.0, halo_ref[...])
    ext = jnp.concatenate([halo, cur], axis=0)
    t1 = (tile * MERGE_TILE + 1 + lax.broadcasted_iota(jnp.int32, (MERGE_TILE, 1), 0)).astype(_F32)
    mixed = []
    win_sum, width = ext, 1
    for g, w in enumerate(POOL_WINDOWS):
        while width < w:
            win_sum = win_sum + pltpu.roll(win_sum, width, 0)
            width *= 2
        cols = slice(g * POOL_GROUP, (g + 1) * POOL_GROUP)
        pooled = win_sum[POOL_HALO:, cols] / jnp.minimum(t1, float(w)) - cur[:, cols]
        mixed.append(_dot(pooled.astype(_BF16), pw_ref[g]))
    mixed = (jnp.concatenate(mixed, axis=-1) * ps_ref[...]).astype(_BF16)
    merged = (sga_ref[...].astype(_F32) * _dot(attn_ref[...], wba_ref[...])
              + sgb_ref[...].astype(_F32) * _dot(mixed, wbp_ref[...]))
    x = x_ref[...] + _dot(merged.astype(_BF16), wo_ref[...])
    o_ref[...] = _ffn_body(x, g_ref, wg_ref, wu_ref, wd_ref, gf_ref, final_norm)


def _merge_ffn(x, attn, pool, sga, sgb, wba, pw, ps, wbp, wo, g, wg, wu, wd, gf, final_norm, seq):
    n = x.shape[0]
    tps = seq // MERGE_TILE
    halo_blocks = MERGE_TILE // POOL_HALO

    def flat(width):
        return pl.BlockSpec((MERGE_TILE, width), lambda i: (i, 0))

    return pl.pallas_call(
        functools.partial(_merge_ffn_kernel, tiles_per_seq=tps, final_norm=final_norm),
        grid=(n // MERGE_TILE,),
        in_specs=[
            flat(D_MODEL), flat(N_HEADS * HEAD_DIM), flat(POOL_WIDTH),
            pl.BlockSpec((POOL_HALO, POOL_WIDTH), lambda i: (jnp.maximum(i * halo_blocks - 1, 0), 0)),
            flat(D_MODEL), flat(D_MODEL),
            _const_spec((N_HEADS * HEAD_DIM, D_MODEL)), _const_spec((len(POOL_WINDOWS), POOL_GROUP, POOL_GROUP)),
            _const_spec((1, POOL_WIDTH)), _const_spec((POOL_WIDTH, D_MODEL)), _const_spec((D_MODEL, D_MODEL)),
            _const_spec((1, D_MODEL)), _const_spec((D_MODEL, D_FF)), _const_spec((D_MODEL, D_FF)),
            _const_spec((D_FF, D_MODEL)), _const_spec((1, D_MODEL)),
        ],
        out_specs=flat(D_MODEL),
        out_shape=jax.ShapeDtypeStruct((n, D_MODEL), _F32),
        compiler_params=pltpu.CompilerParams(dimension_semantics=("arbitrary",), vmem_limit_bytes=VMEM_LIMIT_BYTES),
        name="merge_ffn",
    )(x, attn, pool, pool, sga, sgb, wba, pw, ps, wbp, wo, g, wg, wu, wd, gf)


def _split_w_in(w):
    cuts, off = {}, 0
    for name, width in (("q", N_HEADS * HEAD_DIM), ("ckv", KV_LATENT), ("qi", IDX_HEADS * IDX_DIM), ("ki", IDX_DIM),
                        ("wi", IDX_HEADS), ("pool", POOL_WIDTH), ("ga", D_MODEL), ("gb", D_MODEL)):
        cuts[name] = w[:, off:off + width]
        off += width
    pad_n = jnp.zeros((w.shape[0], LANES - IDX_DIM), w.dtype)
    wn = jnp.concatenate([cuts["ckv"], cuts["pool"], cuts["ga"], cuts["gb"], cuts["ki"], pad_n], axis=1)
    pad_t = jnp.zeros((w.shape[0], BF16_SUBLANES - IDX_HEADS), w.dtype)
    wt = jnp.concatenate([cuts["q"], cuts["qi"], cuts["wi"], pad_t], axis=1).T
    return wn.astype(_BF16), wt.astype(_BF16)


def kernel(x, norm_ffn1, ffn1_gate, ffn1_up, ffn1_down, norm_mix, w_in, norm_kv, w_uk, w_uv, pool_w, pool_scale,
           w_branch_attn, w_branch_pool, w_out, norm_ffn2, ffn2_gate, ffn2_up, ffn2_down, norm_final):
    batch, seq, _ = x.shape
    depth = norm_ffn1.shape[0]
    assert seq % TOKEN_TILE == 0 and seq % MERGE_TILE == 0 and seq % Q_TILE == 0 and batch % SEQ_PER_STEP == 0
    k_sel = min(TOPK_MAX, seq // 4)
    n = batch * seq
    bf = lambda a: a.astype(_BF16)
    row = lambda a: a.reshape(1, -1)
    gf = row(norm_final)
    h = x.reshape(n, D_MODEL)
    for i in range(depth):
        h = _ffn(h, row(norm_ffn1[i]), bf(ffn1_gate[i]), bf(ffn1_up[i]), bf(ffn1_down[i]), gf, False)
        wn, wt = _split_w_in(w_in[i])
        wuk_h = bf(jnp.transpose(w_uk[i], (1, 0, 2)))
        wuvt_h = bf(jnp.transpose(w_uv[i], (1, 2, 0)))
        qlt, ckv, ckvt, qit, ki, wit, pool, sga, sgb = _proj(h, row(norm_mix[i]), wn, wt, row(norm_kv[i]), wuk_h,
                                                             batch, seq)
        attn = _attn(qlt, qit, wit, ki.reshape(batch, seq, -1), ckv.reshape(batch, seq, -1), ckvt, wuvt_h, k_sel)
        h = _merge_ffn(h, attn.reshape(n, -1), pool, sga, sgb, bf(w_branch_attn[i]), bf(pool_w[i]),
                       row(pool_scale[i]), bf(w_branch_pool[i]), bf(w_out[i]), row(norm_ffn2[i]), bf(ffn2_gate[i]),
                       bf(ffn2_up[i]), bf(ffn2_down[i]), gf, i == depth - 1, seq)
    return h.reshape(batch, seq, D_MODEL)
```
